```python
import jax, jax.numpy as jnp
from jax import lax
import numpy as np

D_MODEL = 1024
BATCH = 8
SEQ = 4096
DEPTH = 4
DEC_BATCH = 4
DEC_SEQ = 4096
PAST_LEN = 128

D_MIX = D_MODEL
ATT_HEADS = 8
Q_LORA = 256
KV_LORA = 128
NOPE_DIM = 64
ROPE_DIM = 32
V_DIM = 64
QK_DIM = NOPE_DIM + ROPE_DIM
ATT_WIDTH = ATT_HEADS * V_DIM
HG_HEADS = 4
HG_KEY = 128
HG_VAL = 128
HG_KW = HG_HEADS * HG_KEY
HG_WIDTH = HG_HEADS * HG_VAL
D_FF = 4 * D_MODEL
PLE_DIM = 256
Q_BLOCK = 128
CHUNK = 64
ROPE_THETA = 10000.0
EPS = 1e-6
IN_SIZES = (Q_LORA, KV_LORA, ROPE_DIM, HG_KW, HG_KW, HG_KW, HG_WIDTH, HG_WIDTH)
IN_COLS = sum(IN_SIZES)
IN_SPLITS = tuple(int(s) for s in np.cumsum(IN_SIZES)[:-1])

kernel_name = 'hymba_mla_hgrn2_bidir_encoder'


def rmsnorm(x, gain):
    xf = x.astype(jnp.float32)
    xf = xf * lax.rsqrt(jnp.mean(xf * xf, axis=-1, keepdims=True) + EPS)
    return (xf * gain.astype(jnp.float32)).astype(x.dtype)


def rope_tables(seq_len):
    inv = 1.0 / (ROPE_THETA ** (jnp.arange(0, ROPE_DIM, 2, dtype=jnp.float32) / ROPE_DIM))
    ang = jnp.arange(seq_len, dtype=jnp.float32)[:, None] * inv[None, :]
    return jnp.cos(ang), jnp.sin(ang)


def apply_rope(x, cos, sin):
    c = cos[None, :, None, :].astype(x.dtype)
    s = sin[None, :, None, :].astype(x.dtype)
    x1, x2 = x[..., :ROPE_DIM // 2], x[..., ROPE_DIM // 2:]
    return jnp.concatenate([x1 * c - x2 * s, x2 * c + x1 * s], axis=-1)


def mla(cq, ckv, kr, q_norm, w_uq, kv_norm, w_ukv, cos, sin):
    B, S, _ = cq.shape
    q = (rmsnorm(cq, q_norm) @ w_uq).reshape(B, S, ATT_HEADS, QK_DIM)
    q = jnp.concatenate([q[..., :NOPE_DIM], apply_rope(q[..., NOPE_DIM:], cos, sin)], axis=-1)
    q = q * jnp.asarray(QK_DIM ** -0.5, q.dtype)
    kv = (rmsnorm(ckv, kv_norm) @ w_ukv).reshape(B, S, ATT_HEADS, NOPE_DIM + V_DIM)
    k_nope, v = kv[..., :NOPE_DIM], kv[..., NOPE_DIM:]
    k_rope = apply_rope(kr[:, :, None, :], cos, sin)
    k = jnp.concatenate([k_nope, jnp.broadcast_to(k_rope, (B, S, ATT_HEADS, ROPE_DIM))], axis=-1)
    nb = S // Q_BLOCK
    qb = q.reshape(B, nb, Q_BLOCK, ATT_HEADS, QK_DIM).transpose(1, 0, 2, 3, 4)

    def block(qblk):
        s = jnp.einsum('bqhd,bkhd->bhqk', qblk, k, preferred_element_type=jnp.float32)
        p = jax.nn.softmax(s, axis=-1).astype(v.dtype)
        return jnp.einsum('bhqk,bkhd->bqhd', p, v)

    o = lax.map(block, qb)
    return o.transpose(1, 0, 2, 3, 4).reshape(B, S, ATT_WIDTH)


def hgrn_direction(q, f_pre, v, lb):
    B, S, H, K = q.shape
    V = v.shape[-1]
    nc = S // CHUNK
    lbh = lb.astype(jnp.float32).reshape(H, K)
    f = lbh + (1.0 - lbh) * jax.nn.sigmoid(f_pre.astype(jnp.float32))
    logf = jnp.log(f)
    kk = 1.0 - f

    def to_chunks(t):
        return t.astype(jnp.float32).reshape(B, nc, CHUNK, H, t.shape[-1]).transpose(1, 0, 3, 2, 4)

    xs = (to_chunks(q), to_chunks(kk), to_chunks(v), to_chunks(logf))
    mask = jnp.tril(jnp.ones((CHUNK, CHUNK), dtype=bool))[:, :, None]

    def step(state, inp):
        qc, kc, vc, gc = inp
        b = jnp.cumsum(gc, axis=2)
        diff = jnp.where(mask, b[:, :, :, None, :] - b[:, :, None, :, :], -jnp.inf)
        scores = jnp.sum(qc[:, :, :, None, :] * kc[:, :, None, :, :] * jnp.exp(diff), axis=-1)
        o = (jnp.einsum('bhts,bhsv->bhtv', scores, vc)
             + jnp.einsum('bhtk,bhkv->bhtv', qc * jnp.exp(b), state))
        b_last = b[:, :, -1:, :]
        state = (jnp.exp(b_last[:, :, 0, :])[..., None] * state
                 + jnp.einsum('bhsk,bhsv->bhkv', kc * jnp.exp(b_last - b), vc))
        return state, o

    s0 = jnp.zeros((B, H, K, V), jnp.float32)
    _, o = lax.scan(step, s0, xs)
    return o.transpose(1, 0, 3, 2, 4).reshape(B, S, H, V)


def hgrn2(hq, hf_fwd, hf_bwd, hi, hg, lb_fwd, lb_bwd, out_norm):
    B, S, _ = hq.shape
    q = jax.nn.silu(hq).reshape(B, S, HG_HEADS, HG_KEY)
    v = hi.reshape(B, S, HG_HEADS, HG_VAL)
    ff = hf_fwd.reshape(B, S, HG_HEADS, HG_KEY)
    fb = hf_bwd.reshape(B, S, HG_HEADS, HG_KEY)
    rev = lambda t: jnp.flip(t, axis=1)
    o = hgrn_direction(q, ff, v, lb_fwd) + rev(hgrn_direction(rev(q), rev(fb), rev(v), lb_bwd))
    o = rmsnorm(o, out_norm.reshape(HG_HEADS, HG_VAL)).astype(hq.dtype)
    return o.reshape(B, S, HG_WIDTH) * jax.nn.silu(hg)


def trunk(x, p, lbs, norm_mix_pre, w_in, q_norm, w_uq, kv_norm, w_ukv, att_out_norm, hg_out_norm,
          w_o, norm_mix_post, norm_mlp_pre, w_up, w_down, norm_mlp_post, w_ple, w_ple_gate, norm_ple):
    S = x.shape[1]
    cos, sin = rope_tables(S)
    for i in range(DEPTH):
        h = rmsnorm(x, norm_mix_pre[i])
        z = h @ w_in[i]
        cq, ckv, kr, hq, hff, hfb, hi, hg = jnp.split(z, IN_SPLITS, axis=-1)
        a = rmsnorm(mla(cq, ckv, kr, q_norm[i], w_uq[i], kv_norm[i], w_ukv[i], cos, sin), att_out_norm[i])
        r = hgrn2(hq, hff, hfb, hi, hg, lbs[0, i], lbs[1, i], hg_out_norm[i])
        mix = jnp.concatenate([a, r], axis=-1) @ w_o[i]
        x = x + rmsnorm(mix, norm_mix_post[i])
        h = rmsnorm(x, norm_mlp_pre[i])
        m = jnp.square(jax.nn.relu(h @ w_up[i])) @ w_down[i]
        x = x + rmsnorm(m, norm_mlp_post[i])
        e = p[i] @ w_ple[i]
        gate = jax.nn.sigmoid(x @ w_ple_gate[i])
        x = x + rmsnorm(e * gate, norm_ple[i])
    return x


def setup_inputs(seed: int = 0) -> dict:
    key = jax.random.key(seed)
    ks = jax.random.split(key, 22)
    f32 = jnp.float32

    def nrm(k, shape, scale):
        return jax.random.normal(k, shape, f32) * scale

    def gain(k, shape):
        return 1.0 + 0.01 * jax.random.normal(k, shape, f32)

    return {
        'x_prompt': nrm(ks[0], (BATCH, SEQ, D_MODEL), 1.0),
        'x_sample': nrm(ks[1], (DEC_BATCH, DEC_SEQ, D_MODEL), 1.0),
        'p_prompt': nrm(ks[2], (DEPTH, BATCH, SEQ, PLE_DIM), 1.0),
        'p_sample': nrm(ks[3], (DEPTH, DEC_BATCH, DEC_SEQ, PLE_DIM), 1.0),
        'norm_mix_pre': gain(ks[4], (DEPTH, D_MODEL)),
        'w_in': nrm(ks[5], (DEPTH, D_MODEL, IN_COLS), D_MODEL ** -0.5),
        'q_norm': gain(ks[6], (DEPTH, Q_LORA)),
        'w_uq': nrm(ks[7], (DEPTH, Q_LORA, ATT_HEADS * QK_DIM), Q_LORA ** -0.5),
        'kv_norm': gain(ks[8], (DEPTH, KV_LORA)),
        'w_ukv': nrm(ks[9], (DEPTH, KV_LORA, ATT_HEADS * (NOPE_DIM + V_DIM)), KV_LORA ** -0.5),
        'att_out_norm': gain(ks[10], (DEPTH, ATT_WIDTH)),
        'hg_lb': nrm(ks[11], (2, DEPTH, HG_KW), 0.5),
        'hg_out_norm': gain(ks[12], (DEPTH, HG_WIDTH)),
        'w_o': nrm(ks[13], (DEPTH, D_MIX, D_MODEL), D_MIX ** -0.5),
        'norm_mix_post': gain(ks[14], (DEPTH, D_MODEL)),
        'norm_mlp_pre': gain(ks[15], (DEPTH, D_MODEL)),
        'w_up': nrm(ks[16], (DEPTH, D_MODEL, D_FF), D_MODEL ** -0.5),
        'w_down': nrm(ks[17], (DEPTH, D_FF, D_MODEL), D_FF ** -0.5),
        'norm_mlp_post': gain(ks[18], (DEPTH, D_MODEL)),
        'w_ple': nrm(ks[19], (DEPTH, PLE_DIM, D_MODEL), PLE_DIM ** -0.5),
        'w_ple_gate': nrm(ks[20], (DEPTH, D_MODEL, D_MODEL), D_MODEL ** -0.5),
        'norm_ple': gain(ks[21], (DEPTH, D_MODEL)),
    }


def reference(x_prompt, x_sample, p_prompt, p_sample, norm_mix_pre, w_in, q_norm, w_uq, kv_norm, w_ukv,
              att_out_norm, hg_lb, hg_out_norm, w_o, norm_mix_post, norm_mlp_pre, w_up, w_down,
              norm_mlp_post, w_ple, w_ple_gate, norm_ple):
    lbs = jnp.cumsum(jax.nn.softmax(hg_lb.astype(jnp.float32), axis=1), axis=1)
    lbs = lbs - lbs[:, :1]
    y_prompt = trunk(x_prompt, p_prompt, lbs, norm_mix_pre, w_in, q_norm, w_uq, kv_norm, w_ukv,
                     att_out_norm, hg_out_norm, w_o, norm_mix_post, norm_mlp_pre, w_up, w_down,
                     norm_mlp_post, w_ple, w_ple_gate, norm_ple)
    y_sample = trunk(x_sample, p_sample, lbs, norm_mix_pre, w_in, q_norm, w_uq, kv_norm, w_ukv,
                     att_out_norm, hg_out_norm, w_o, norm_mix_post, norm_mlp_pre, w_up, w_down,
                     norm_mlp_post, w_ple, w_ple_gate, norm_ple)
    return (y_prompt, y_sample)
```

```python
import functools

import numpy as np
import jax
import jax.numpy as jnp
from jax import lax
from jax.experimental import pallas as pl
from jax.experimental.pallas import tpu as pltpu

D_MODEL = 1024
DEPTH = 4
ATT_HEADS = 8
Q_LORA = 256
KV_LORA = 128
NOPE_DIM = 64
ROPE_DIM = 32
V_DIM = 64
QK_DIM = NOPE_DIM + ROPE_DIM
ATT_WIDTH = ATT_HEADS * V_DIM
HG_HEADS = 4
HG_KEY = 128
HG_VAL = 128
HG_W = HG_HEADS * HG_KEY
D_FF = 4 * D_MODEL
PLE_DIM = 256
ROPE_THETA = 10000.0
EPS = 1e-6

LANES = 128
HEAD_SLAB = LANES
QK_SLAB = ATT_HEADS * HEAD_SLAB
HG_CHUNK = 128
VMEM_LIMIT = 56 * 1024 * 1024

F32 = jnp.float32
BF16 = jnp.bfloat16

_C_CQ = 0
_C_CKV = _C_CQ + Q_LORA
_C_KR = _C_CKV + KV_LORA
_C_KRS = _C_KR + LANES
_C_HQ = _C_KRS + LANES
_C_HFF = _C_HQ + HG_W
_C_HFB = _C_HFF + HG_W
_C_HI = _C_HFB + HG_W
_C_HG = _C_HI + HG_W
_IN_COLS_P = _C_HG + HG_W


def _rms(x, gain):
    return x * lax.rsqrt(jnp.mean(x * x, axis=-1, keepdims=True) + EPS) * gain


def _dot(a, b):
    return jnp.dot(a, b, preferred_element_type=F32)


def _dot_nt(a, b):
    return lax.dot_general(a, b, (((1,), (1,)), ((), ())), preferred_element_type=F32)


def _dot_tn(a, b):
    return lax.dot_general(a, b, (((0,), (0,)), ((), ())), preferred_element_type=F32)


def _const_spec(shape):
    return pl.BlockSpec(shape, lambda *_: (0,) * len(shape))


def _in_proj_kernel(x_ref, g_ref, win_ref, qn_ref, kvn_ref, wuq_ref, wukv_ref, e1_ref,
                    cq_ref, sq_ref, ck_ref, sk_ref, lbf_ref, lbb_ref,
                    q_out, k_out, v_out, hq_out, kkf_out, kkb_out, gf_out, gb_out, hv_out, gate_out):
    h = _rms(x_ref[...], g_ref[...]).astype(BF16)
    z = _dot(h, win_ref[...])

    qn = _rms(z[:, _C_CQ:_C_CKV], qn_ref[...]).astype(BF16)
    q2 = _dot(qn, wuq_ref[...])
    cq = cq_ref[...]
    sq = sq_ref[...]
    for hd in range(ATT_HEADS):
        a = q2[:, hd * HEAD_SLAB:(hd + 1) * HEAD_SLAB]
        b = q2[:, QK_SLAB + hd * HEAD_SLAB:QK_SLAB + (hd + 1) * HEAD_SLAB]
        q_out[:, hd * HEAD_SLAB:(hd + 1) * HEAD_SLAB] = (a * cq + b * sq).astype(BF16)

    kvn = _rms(z[:, _C_CKV:_C_KR], kvn_ref[...]).astype(BF16)
    kv2 = _dot(kvn, wukv_ref[...])
    kr = z[:, _C_KR:_C_KRS] * ck_ref[...] + z[:, _C_KRS:_C_HQ] * sk_ref[...]
    krp = _dot(kr.astype(BF16), e1_ref[...])
    k_out[...] = (kv2[:, :QK_SLAB] + krp).astype(BF16)
    v_out[...] = kv2[:, QK_SLAB:].astype(BF16)

    hq = z[:, _C_HQ:_C_HFF]
    hq_out[...] = (hq * jax.nn.sigmoid(hq)).astype(BF16)
    for src, lb_ref, kk_o, g_o in ((_C_HFF, lbf_ref, kkf_out, gf_out), (_C_HFB, lbb_ref, kkb_out, gb_out)):
        lb = lb_ref[...]
        f = lb + (1.0 - lb) * jax.nn.sigmoid(z[:, src:src + HG_W])
        g_o[...] = jnp.log(f)
        kk_o[...] = (1.0 - f).astype(BF16)
    hv_out[...] = z[:, _C_HI:_C_HG].astype(BF16)
    hg = z[:, _C_HG:_IN_COLS_P]
    gate_out[...] = (hg * jax.nn.sigmoid(hg)).astype(BF16)


def _in_proj(x, gain, win, qn, kvn, wuq, wukv, e1, cq, sq, ck, sk, lbf, lbb, *, seq, tm):
    n = x.shape[0]
    nt = seq // tm
    row = lambda w: pl.BlockSpec((tm, w), lambda i: (i, 0))
    tab = pl.BlockSpec((tm, LANES), lambda i: (i % nt, 0))
    outs = [(QK_SLAB, BF16), (QK_SLAB, BF16), (ATT_WIDTH, BF16), (HG_W, BF16), (HG_W, BF16), (HG_W, BF16),
            (HG_W, F32), (HG_W, F32), (HG_W, BF16), (HG_W, BF16)]
    return pl.pallas_call(
        _in_proj_kernel,
        grid=(n // tm,),
        in_specs=[row(D_MODEL), _const_spec((1, D_MODEL)), _const_spec(win.shape),
                  _const_spec((1, Q_LORA)), _const_spec((1, KV_LORA)), _const_spec(wuq.shape),
                  _const_spec(wukv.shape), _const_spec(e1.shape), tab, tab, tab, tab,
                  _const_spec((1, HG_W)), _const_spec((1, HG_W))],
        out_specs=[row(w) for w, _ in outs],
        out_shape=[jax.ShapeDtypeStruct((n, w), dt) for w, dt in outs],
        compiler_params=pltpu.CompilerParams(dimension_semantics=("parallel",), vmem_limit_bytes=VMEM_LIMIT),
        name="in_proj",
    )(x, gain, win, qn, kvn, wuq, wukv, e1, cq, sq, ck, sk, lbf, lbb)


def _attention_kernel(q_ref, k_ref, v_ref, norm_ref, a_out, o_scr):
    tq = q_ref.shape[0]
    first_half = lax.broadcasted_iota(jnp.int32, (tq, LANES), 1) < V_DIM
    for pair in range(ATT_HEADS // 2):
        vp = v_ref[:, pair * LANES:(pair + 1) * LANES]
        outs = []
        for j in range(2):
            hd = 2 * pair + j
            qh = q_ref[:, hd * HEAD_SLAB:(hd + 1) * HEAD_SLAB]
            kh = k_ref[:, hd * HEAD_SLAB:(hd + 1) * HEAD_SLAB]
            s = _dot_nt(qh, kh)
            m = jnp.max(s, axis=-1, keepdims=True)
            p = jnp.exp(s - m)
            l = jnp.sum(p, axis=-1, keepdims=True)
            outs.append(_dot(p.astype(BF16), vp) / l)
        o_scr[:, pair * LANES:(pair + 1) * LANES] = jnp.where(first_half, outs[0], outs[1])
    a_out[...] = _rms(o_scr[...], norm_ref[...]).astype(BF16)


def _attention(q, k, v, norm, *, tq):
    bsz, seq, _ = q.shape
    return pl.pallas_call(
        _attention_kernel,
        grid=(bsz, seq // tq),
        in_specs=[pl.BlockSpec((None, tq, QK_SLAB), lambda b, i: (b, i, 0)),
                  pl.BlockSpec((None, seq, QK_SLAB), lambda b, i: (b, 0, 0)),
                  pl.BlockSpec((None, seq, ATT_WIDTH), lambda b, i: (b, 0, 0)),
                  _const_spec((1, ATT_WIDTH))],
        out_specs=pl.BlockSpec((None, tq, ATT_WIDTH), lambda b, i: (b, i, 0)),
        out_shape=jax.ShapeDtypeStruct((bsz, seq, ATT_WIDTH), BF16),
        scratch_shapes=[pltpu.VMEM((tq, ATT_WIDTH), F32)],
        compiler_params=pltpu.CompilerParams(dimension_semantics=("parallel", "arbitrary"),
                                             vmem_limit_bytes=VMEM_LIMIT),
        name="attention",
    )(q, k, v, norm)


_HG_LEVELS = int(np.log2(HG_CHUNK))
_HG_DIAG = _HG_LEVELS


@functools.lru_cache(maxsize=None)
def _hgrn_consts():
    c = HG_CHUNK
    t = np.arange(c)[:, None]
    u = np.arange(c)[None, :]
    blocks = [u <= t, u > t]
    level = np.full((c, c), -1, np.int32)
    level[np.arange(c), np.arange(c)] = _HG_DIAG
    for li in range(_HG_LEVELS):
        h = (c // 2) >> li
        mid = (t // (2 * h)) * (2 * h) + h - 1
        upper = (t % (2 * h)) >= h
        blocks.append((upper & (u > mid) & (u <= t)) | (~upper & (u > t) & (u <= mid)))
        same = (t // (2 * h)) == (u // (2 * h))
        level[same & upper & ((u % (2 * h)) < h)] = li
    m_f = np.concatenate(blocks, 0).astype(np.float32)
    m_b = np.concatenate([blk[::-1, ::-1] for blk in blocks], 0).astype(np.float32)
    return m_f, m_b, level, np.ascontiguousarray(level.T)


def _hgrn_chunk(rows, backward, m_ref, lv_ref, q_ref, kk_ref, g_ref, v_ref, st):
    c = HG_CHUNK
    g = g_ref[rows, :]
    g_hi = g.astype(BF16)
    g_lo = (g - g_hi.astype(F32)).astype(BF16)
    r = _dot(m_ref[...], jnp.concatenate([g_hi, g_lo], axis=1))
    x = r[:, :HG_KEY] + r[:, HG_KEY:]
    b = x[0:c]
    q = q_ref[rows, :].astype(F32)
    kk = kk_ref[rows, :].astype(F32)
    v = v_ref[rows, :]
    lv = lv_ref[...]
    tbit = lax.broadcasted_iota(jnp.int32, (c, HG_KEY), 0)

    scores = jnp.where(lv == _HG_DIAG, _dot_nt(q.astype(BF16), kk.astype(BF16)), 0.0)
    for li in range(_HG_LEVELS):
        h = (c // 2) >> li
        is_query = ((tbit & h) == 0) if backward else ((tbit & h) != 0)
        p = (jnp.where(is_query, q, kk) * jnp.exp(x[(2 + li) * c:(3 + li) * c])).astype(BF16)
        scores = jnp.where(lv == li, _dot_nt(p, p), scores)

    o = _dot(scores.astype(BF16), v)
    o = o + _dot_nt((q * jnp.exp(b)).astype(BF16), st.astype(BF16))
    kd = (kk * jnp.exp(x[c:2 * c])).astype(BF16)
    b_end = b[0:1] if backward else b[c - 1:c]
    st_new = st * jnp.exp(b_end) + _dot_tn(v, kd)
    return o, st_new


def _hgrn_kernel(mf_ref, mb_ref, lvf_ref, lvb_ref, q_ref, kkf_ref, kkb_ref, gf_ref, gb_ref, v_ref,
                 gate_ref, norm_ref, r_out, acc):
    c = HG_CHUNK
    nc = q_ref.shape[0] // c
    half = nc // 2

    def both(i, st_f, st_b):
        rows_f = pl.ds(pl.multiple_of(i * c, c), c)
        rows_b = pl.ds(pl.multiple_of((nc - 1 - i) * c, c), c)
        o_f, st_f = _hgrn_chunk(rows_f, False, mf_ref, lvf_ref, q_ref, kkf_ref, gf_ref, v_ref, st_f)
        o_b, st_b = _hgrn_chunk(rows_b, True, mb_ref, lvb_ref, q_ref, kkb_ref, gb_ref, v_ref, st_b)
        return rows_f, rows_b, o_f, o_b, st_f, st_b

    def first_touch(i, carry):
        rows_f, rows_b, o_f, o_b, st_f, st_b = both(i, *carry)
        acc[rows_f, :] = o_f
        acc[rows_b, :] = o_b
        return st_f, st_b

    def finish(rows, o):
        tot = acc[rows, :] + o
        r_out[rows, :] = (_rms(tot, norm_ref[...]) * gate_ref[rows, :].astype(F32)).astype(BF16)

    def second_touch(i, carry):
        rows_f, rows_b, o_f, o_b, st_f, st_b = both(i, *carry)
        finish(rows_f, o_f)
        finish(rows_b, o_b)
        return st_f, st_b

    zero = jnp.zeros((HG_VAL, HG_KEY), F32)
    carry = lax.fori_loop(0, half, first_touch, (zero, zero))
    lax.fori_loop(half, nc, second_touch, carry)


def _hgrn(qs, kkf, kkb, gf, gb, hv, gate, norm):
    bsz, seq, _ = qs.shape
    assert seq % (2 * HG_CHUNK) == 0
    m_f, m_b, lv_f, lv_b = _hgrn_consts()
    head = pl.BlockSpec((None, seq, HG_KEY), lambda b, h: (b, 0, h))
    return pl.pallas_call(
        _hgrn_kernel,
        grid=(bsz, HG_HEADS),
        in_specs=[_const_spec(m_f.shape), _const_spec(m_b.shape), _const_spec(lv_f.shape), _const_spec(lv_b.shape),
                  head, head, head, head, head, head, head,
                  pl.BlockSpec((1, HG_VAL), lambda b, h: (0, h))],
        out_specs=head,
        out_shape=jax.ShapeDtypeStruct((bsz, seq, HG_W), BF16),
        scratch_shapes=[pltpu.VMEM((seq, HG_VAL), F32)],
        compiler_params=pltpu.CompilerParams(dimension_semantics=("parallel", "parallel"),
                                             vmem_limit_bytes=VMEM_LIMIT),
        name="hgrn",
    )(jnp.asarray(m_f, BF16), jnp.asarray(m_b, BF16), jnp.asarray(lv_f), jnp.asarray(lv_b),
      qs, kkf, kkb, gf, gb, hv, gate, norm)


_FF_CHUNK = 1024


def _post_kernel(x_ref, a_ref, r_ref, p_ref, wo_ref, n1_ref, n2_ref, wup_ref, wdn_ref, n3_ref,
                 wple_ref, wpg_ref, n4_ref, x_out):
    mix = _dot(a_ref[...], wo_ref[:ATT_WIDTH, :]) + _dot(r_ref[...], wo_ref[ATT_WIDTH:, :])
    x = x_ref[...] + _rms(mix, n1_ref[...])
    h = _rms(x, n2_ref[...]).astype(BF16)
    m = jnp.zeros_like(x)
    for c0 in range(0, D_FF, _FF_CHUNK):
        act = jnp.square(jnp.maximum(_dot(h, wup_ref[:, c0:c0 + _FF_CHUNK]), 0.0)).astype(BF16)
        m = m + _dot(act, wdn_ref[c0:c0 + _FF_CHUNK, :])
    x = x + _rms(m, n3_ref[...])
    e = _dot(p_ref[...].astype(BF16), wple_ref[...])
    gate = jax.nn.sigmoid(_dot(x.astype(BF16), wpg_ref[...]))
    x_out[...] = x + _rms(e * gate, n4_ref[...])


def _post(x, a, r, p, wo, n1, n2, wup, wdn, n3, wple, wpg, n4, *, tm):
    n = x.shape[0]
    row = lambda w: pl.BlockSpec((tm, w), lambda i: (i, 0))
    vec = _const_spec((1, D_MODEL))
    single = lambda w: pl.BlockSpec(w.shape, lambda i: (0, 0), pipeline_mode=pl.Buffered(1))
    return pl.pallas_call(
        _post_kernel,
        grid=(n // tm,),
        in_specs=[row(D_MODEL), row(ATT_WIDTH), row(HG_W), row(PLE_DIM), single(wo), vec, vec,
                  single(wup), single(wdn), vec, single(wple), single(wpg), vec],
        out_specs=row(D_MODEL),
        out_shape=jax.ShapeDtypeStruct((n, D_MODEL), F32),
        compiler_params=pltpu.CompilerParams(dimension_semantics=("parallel",), vmem_limit_bytes=VMEM_LIMIT),
        name="post",
    )(x, a, r, p, wo, n1, n2, wup, wdn, n3, wple, wpg, n4)


def _layout_w_in(w):
    cq_ckv = w[:, :Q_LORA + KV_LORA]
    kr = w[:, Q_LORA + KV_LORA:Q_LORA + KV_LORA + ROPE_DIM]
    rest = w[:, Q_LORA + KV_LORA + ROPE_DIM:]
    half = ROPE_DIM // 2
    pad = jnp.zeros((w.shape[0], LANES - ROPE_DIM), w.dtype)
    kr_swap = jnp.concatenate([-kr[:, half:], kr[:, :half]], axis=1)
    return jnp.concatenate([cq_ckv, kr, pad, kr_swap, pad, rest], axis=1)


def _layout_w_uq(w):
    half = ROPE_DIM // 2
    w = w.reshape(Q_LORA, ATT_HEADS, QK_DIM)
    nope, x1, x2 = w[..., :NOPE_DIM], w[..., NOPE_DIM:NOPE_DIM + half], w[..., NOPE_DIM + half:]
    pad = jnp.zeros((Q_LORA, ATT_HEADS, HEAD_SLAB - QK_DIM), w.dtype)
    plain = jnp.concatenate([nope, x1, x2, pad], axis=-1)
    swap = jnp.concatenate([jnp.zeros_like(nope), -x2, x1, pad], axis=-1)
    return jnp.concatenate([plain.reshape(Q_LORA, QK_SLAB), swap.reshape(Q_LORA, QK_SLAB)], axis=1)


def _layout_w_ukv(w):
    w = w.reshape(KV_LORA, ATT_HEADS, NOPE_DIM + V_DIM)
    pad = jnp.zeros((KV_LORA, ATT_HEADS, HEAD_SLAB - NOPE_DIM), w.dtype)
    k = jnp.concatenate([w[..., :NOPE_DIM], pad], axis=-1).reshape(KV_LORA, QK_SLAB)
    v = w[..., NOPE_DIM:].reshape(KV_LORA, ATT_WIDTH)
    return jnp.concatenate([k, v], axis=1)


def _rope_placement():
    e = np.zeros((LANES, QK_SLAB), np.float32)
    for hd in range(ATT_HEADS):
        e[np.arange(ROPE_DIM), hd * HEAD_SLAB + NOPE_DIM + np.arange(ROPE_DIM)] = 1.0
    return e


def _rope_tables(seq):
    inv = 1.0 / (ROPE_THETA ** (jnp.arange(0, ROPE_DIM, 2, dtype=F32) / ROPE_DIM))
    ang = jnp.arange(seq, dtype=F32)[:, None] * inv[None, :]
    cos, sin = jnp.cos(ang), jnp.sin(ang)
    scale = jnp.asarray(QK_DIM ** -0.5, F32)
    z = lambda w: jnp.zeros((seq, w), F32)
    cq = jnp.concatenate([jnp.ones((seq, NOPE_DIM), F32), cos, cos, z(HEAD_SLAB - QK_DIM)], axis=1) * scale
    sq = jnp.concatenate([z(NOPE_DIM), sin, sin, z(HEAD_SLAB - QK_DIM)], axis=1) * scale
    ck = jnp.concatenate([cos, cos, z(LANES - ROPE_DIM)], axis=1)
    sk = jnp.concatenate([sin, sin, z(LANES - ROPE_DIM)], axis=1)
    return cq, sq, ck, sk


def _trunk(x, p, lbs, norm_mix_pre, w_in, q_norm, w_uq, kv_norm, w_ukv, att_out_norm, hg_out_norm, w_o,
           norm_mix_post, norm_mlp_pre, w_up, w_down, norm_mlp_post, w_ple, w_ple_gate, norm_ple):
    bsz, seq, _ = x.shape
    n = bsz * seq
    tm = min(512, seq)
    tq = min(256, seq)
    cq, sq, ck, sk = _rope_tables(seq)
    e1 = jnp.asarray(_rope_placement(), BF16)
    row = lambda v: v.reshape(1, -1).astype(F32)
    xf = x.reshape(n, D_MODEL)
    for i in range(DEPTH):
        q, k, v, hq, kkf, kkb, gf, gb, hv, gate = _in_proj(
            xf, row(norm_mix_pre[i]), _layout_w_in(w_in[i]).astype(BF16), row(q_norm[i]), row(kv_norm[i]),
            _layout_w_uq(w_uq[i]).astype(BF16), _layout_w_ukv(w_ukv[i]).astype(BF16), e1,
            cq, sq, ck, sk, row(lbs[0, i]), row(lbs[1, i]), seq=seq, tm=tm)
        b3 = lambda t: t.reshape(bsz, seq, t.shape[-1])
        a = _attention(b3(q), b3(k), b3(v), row(att_out_norm[i]), tq=tq)
        r = _hgrn(b3(hq), b3(kkf), b3(kkb), b3(gf), b3(gb), b3(hv), b3(gate), row(hg_out_norm[i]))
        xf = _post(xf, a.reshape(n, ATT_WIDTH), r.reshape(n, HG_W), p[i].reshape(n, PLE_DIM),
                   w_o[i].astype(BF16), row(norm_mix_post[i]), row(norm_mlp_pre[i]), w_up[i].astype(BF16),
                   w_down[i].astype(BF16), row(norm_mlp_post[i]), w_ple[i].astype(BF16),
                   w_ple_gate[i].astype(BF16), row(norm_ple[i]), tm=tm)
    return xf.reshape(bsz, seq, D_MODEL)


def kernel(x_prompt, x_sample, p_prompt, p_sample, norm_mix_pre, w_in, q_norm, w_uq, kv_norm, w_ukv, att_out_norm, hg_lb, hg_out_norm, w_o, norm_mix_post, norm_mlp_pre, w_up, w_down, norm_mlp_post, w_ple, w_ple_gate, norm_ple):
    lbs = jnp.cumsum(jax.nn.softmax(hg_lb.astype(F32), axis=1), axis=1)
    lbs = lbs - lbs[:, :1]
    assert x_prompt.shape[1:] == x_sample.shape[1:]
    nb = x_prompt.shape[0]
    x = jnp.concatenate([x_prompt, x_sample], axis=0)
    p = jnp.concatenate([p_prompt, p_sample], axis=1)
    y = _trunk(x, p, lbs, norm_mix_pre, w_in, q_norm, w_uq, kv_norm, w_ukv, att_out_norm, hg_out_norm, w_o,
               norm_mix_post, norm_mlp_pre, w_up, w_down, norm_mlp_post, w_ple, w_ple_gate, norm_ple)
    return (y[:nb], y[nb:])
```

```python
import functools

import numpy as np
import jax
import jax.numpy as jnp
from jax import lax
from jax.experimental import pallas as pl
from jax.experimental.pallas import tpu as pltpu

D_MODEL = 1024
DEPTH = 4
ATT_HEADS = 8
Q_LORA = 256
KV_LORA = 128
NOPE_DIM = 64
ROPE_DIM = 32
V_DIM = 64
QK_DIM = NOPE_DIM + ROPE_DIM
ATT_WIDTH = ATT_HEADS * V_DIM
HG_HEADS = 4
HG_KEY = 128
HG_VAL = 128
HG_W = HG_HEADS * HG_KEY
D_FF = 4 * D_MODEL
PLE_DIM = 256
ROPE_THETA = 10000.0
EPS = 1e-6

LANES = 128
HEAD_SLAB = LANES
QK_SLAB = ATT_HEADS * HEAD_SLAB
V_ROWS = V_DIM + 16
VT_ROWS = ATT_HEADS * V_ROWS
LOG2E = 1.4426950408889634
HG_CHUNK = 128
VMEM_LIMIT = 56 * 1024 * 1024

F32 = jnp.float32
BF16 = jnp.bfloat16

_C_CQ = 0
_C_CKV = _C_CQ + Q_LORA
_C_KR = _C_CKV + KV_LORA
_C_KRS = _C_KR + LANES
_C_HQ = _C_KRS + LANES
_C_HFF = _C_HQ + HG_W
_C_HFB = _C_HFF + HG_W
_C_HI = _C_HFB + HG_W
_C_HG = _C_HI + HG_W
_IN_COLS_P = _C_HG + HG_W


def _rms(x, gain):
    return x * lax.rsqrt(jnp.mean(x * x, axis=-1, keepdims=True) + EPS) * gain


def _dot(a, b):
    return jnp.dot(a, b, preferred_element_type=F32)


def _dot_nt(a, b):
    return lax.dot_general(a, b, (((1,), (1,)), ((), ())), preferred_element_type=F32)


def _dot_tn(a, b):
    return lax.dot_general(a, b, (((0,), (0,)), ((), ())), preferred_element_type=F32)


def _const_spec(shape):
    return pl.BlockSpec(shape, lambda *_: (0,) * len(shape))


def _in_proj_kernel(x_ref, g_ref, win_ref, qn_ref, kvn_ref, wuq_ref, wukv_ref, e1_ref, ones_ref,
                    cq_ref, sq_ref, ck_ref, sk_ref, lbf_ref, lbb_ref,
                    q_out, k_out, v_out, hq_out, kkf_out, kkb_out, gf_out, gb_out, hv_out, gate_out):
    h = _rms(x_ref[...], g_ref[...]).astype(BF16)
    z = _dot(h, win_ref[...])

    qn = _rms(z[:, _C_CQ:_C_CKV], qn_ref[...]).astype(BF16)
    q2 = _dot(qn, wuq_ref[...])
    cq = cq_ref[...]
    sq = sq_ref[...]
    for hd in range(ATT_HEADS):
        a = q2[:, hd * HEAD_SLAB:(hd + 1) * HEAD_SLAB]
        b = q2[:, QK_SLAB + hd * HEAD_SLAB:QK_SLAB + (hd + 1) * HEAD_SLAB]
        q_out[:, hd * HEAD_SLAB:(hd + 1) * HEAD_SLAB] = (a * cq + b * sq).astype(BF16)

    kvn = _rms(z[:, _C_CKV:_C_KR], kvn_ref[...]).astype(BF16)
    kv2 = _dot(kvn, wukv_ref[...])
    kr = z[:, _C_KR:_C_KRS] * ck_ref[...] + z[:, _C_KRS:_C_HQ] * sk_ref[...]
    krp = _dot(kr.astype(BF16), e1_ref[...])
    k_out[...] = (kv2[:, :QK_SLAB] + krp).astype(BF16)
    v_out[...] = (kv2[:, QK_SLAB:] + ones_ref[...]).T.astype(BF16)

    hq = z[:, _C_HQ:_C_HFF]
    hq_out[...] = (hq * jax.nn.sigmoid(hq)).astype(BF16)
    for src, lb_ref, kk_o, g_o in ((_C_HFF, lbf_ref, kkf_out, gf_out), (_C_HFB, lbb_ref, kkb_out, gb_out)):
        lb = lb_ref[...]
        f = lb + (1.0 - lb) * jax.nn.sigmoid(z[:, src:src + HG_W])
        g_o[...] = jnp.log(f)
        kk_o[...] = (1.0 - f).astype(BF16)
    hv_out[...] = z[:, _C_HI:_C_HG].astype(BF16)
    hg = z[:, _C_HG:_IN_COLS_P]
    gate_out[...] = (hg * jax.nn.sigmoid(hg)).astype(BF16)


def _in_proj(x, gain, win, qn, kvn, wuq, wukv, e1, ones, cq, sq, ck, sk, lbf, lbb, *, seq, tm):
    n = x.shape[0]
    nt = seq // tm
    row = lambda w: pl.BlockSpec((tm, w), lambda i: (i, 0))
    tab = pl.BlockSpec((tm, LANES), lambda i: (i % nt, 0))
    outs = [(QK_SLAB, BF16), (QK_SLAB, BF16), None, (HG_W, BF16), (HG_W, BF16), (HG_W, BF16),
            (HG_W, F32), (HG_W, F32), (HG_W, BF16), (HG_W, BF16)]
    vt_spec = pl.BlockSpec((None, VT_ROWS, tm), lambda i: (i, 0, 0))
    vt_shape = jax.ShapeDtypeStruct((n // tm, VT_ROWS, tm), BF16)
    return pl.pallas_call(
        _in_proj_kernel,
        grid=(n // tm,),
        in_specs=[row(D_MODEL), _const_spec((1, D_MODEL)), _const_spec(win.shape),
                  _const_spec((1, Q_LORA)), _const_spec((1, KV_LORA)), _const_spec(wuq.shape),
                  _const_spec(wukv.shape), _const_spec(e1.shape), _const_spec(ones.shape), tab, tab, tab, tab,
                  _const_spec((1, HG_W)), _const_spec((1, HG_W))],
        out_specs=[vt_spec if o is None else row(o[0]) for o in outs],
        out_shape=[vt_shape if o is None else jax.ShapeDtypeStruct((n, o[0]), o[1]) for o in outs],
        compiler_params=pltpu.CompilerParams(dimension_semantics=("parallel",), vmem_limit_bytes=VMEM_LIMIT),
        name="in_proj",
    )(x, gain, win, qn, kvn, wuq, wukv, e1, ones, cq, sq, ck, sk, lbf, lbb)


def _attention_kernel(q_ref, k_ref, vt_ref, ot_out, st_scr, acc_scr):
    nkb, _, kb = vt_ref.shape
    q = q_ref[...]
    st_scr[0] = _dot_nt(k_ref[0:kb, :], q)
    m = jnp.full((1, q.shape[0]), -jnp.inf, F32)
    for j in range(nkb):
        if j + 1 < nkb:
            st_scr[(j + 1) % 2] = _dot_nt(k_ref[(j + 1) * kb:(j + 2) * kb, :], q)
        st = st_scr[j % 2]
        m_new = jnp.maximum(m, jnp.max(st, axis=0, keepdims=True))
        p = jnp.exp2(st - m_new).astype(BF16)
        pv = _dot(vt_ref[j], p)
        acc_scr[...] = pv if j == 0 else jnp.exp2(m - m_new) * acc_scr[...] + pv
        m = m_new
    acc = acc_scr[...]
    ot_out[...] = (acc[:V_DIM] / acc[V_DIM:V_DIM + 1]).astype(BF16)


def _attention(q, k, vt, *, tq):
    bsz, seq, _ = q.shape
    nkb = vt.shape[0] // bsz
    kb = vt.shape[2]
    return pl.pallas_call(
        _attention_kernel,
        grid=(bsz, ATT_HEADS, seq // tq),
        in_specs=[pl.BlockSpec((None, tq, HEAD_SLAB), lambda b, h, i: (b, i, h)),
                  pl.BlockSpec((None, seq, HEAD_SLAB), lambda b, h, i: (b, 0, h)),
                  pl.BlockSpec((nkb, V_ROWS, kb), lambda b, h, i: (b, h, 0))],
        out_specs=pl.BlockSpec((None, V_DIM, tq), lambda b, h, i: (b, h, i)),
        out_shape=jax.ShapeDtypeStruct((bsz, ATT_WIDTH, seq), BF16),
        scratch_shapes=[pltpu.VMEM((2, kb, tq), F32), pltpu.VMEM((V_ROWS, tq), F32)],
        compiler_params=pltpu.CompilerParams(dimension_semantics=("parallel", "parallel", "arbitrary"),
                                             vmem_limit_bytes=VMEM_LIMIT),
        name="attention",
    )(q, k, vt)


_HG_LEVELS = int(np.log2(HG_CHUNK))
_HG_DIAG = _HG_LEVELS


@functools.lru_cache(maxsize=None)
def _hgrn_consts():
    c = HG_CHUNK
    t = np.arange(c)[:, None]
    u = np.arange(c)[None, :]
    blocks = [u <= t, u > t]
    level = np.full((c, c), -1, np.int32)
    level[np.arange(c), np.arange(c)] = _HG_DIAG
    for li in range(_HG_LEVELS):
        h = (c // 2) >> li
        mid = (t // (2 * h)) * (2 * h) + h - 1
        upper = (t % (2 * h)) >= h
        blocks.append((upper & (u > mid) & (u <= t)) | (~upper & (u > t) & (u <= mid)))
        same = (t // (2 * h)) == (u // (2 * h))
        level[same & upper & ((u % (2 * h)) < h)] = li
    m_f = np.concatenate(blocks, 0).astype(np.float32)
    m_b = np.concatenate([blk[::-1, ::-1] for blk in blocks], 0).astype(np.float32)
    return m_f, m_b, level, np.ascontiguousarray(level.T)


def _hgrn_chunk(rows, backward, m_ref, lv_ref, q_ref, kk_ref, g_ref, v_ref, st):
    c = HG_CHUNK
    g = g_ref[rows, :]
    g_hi = g.astype(BF16)
    g_lo = (g - g_hi.astype(F32)).astype(BF16)
    r = _dot(m_ref[...], jnp.concatenate([g_hi, g_lo], axis=1))
    x = r[:, :HG_KEY] + r[:, HG_KEY:]
    b = x[0:c]
    q = q_ref[rows, :].astype(F32)
    kk = kk_ref[rows, :].astype(F32)
    v = v_ref[rows, :]
    lv = lv_ref[...]
    tbit = lax.broadcasted_iota(jnp.int32, (c, HG_KEY), 0)

    scores = jnp.where(lv == _HG_DIAG, _dot_nt(q.astype(BF16), kk.astype(BF16)), 0.0)
    for li in range(_HG_LEVELS):
        h = (c // 2) >> li
        is_query = ((tbit & h) == 0) if backward else ((tbit & h) != 0)
        p = (jnp.where(is_query, q, kk) * jnp.exp(x[(2 + li) * c:(3 + li) * c])).astype(BF16)
        scores = jnp.where(lv == li, _dot_nt(p, p), scores)

    o = _dot(scores.astype(BF16), v)
    o = o + _dot_nt((q * jnp.exp(b)).astype(BF16), st.astype(BF16))
    kd = (kk * jnp.exp(x[c:2 * c])).astype(BF16)
    b_end = b[0:1] if backward else b[c - 1:c]
    st_new = st * jnp.exp(b_end) + _dot_tn(v, kd)
    return o, st_new


def _hgrn_kernel(mf_ref, mb_ref, lvf_ref, lvb_ref, q_ref, kkf_ref, kkb_ref, gf_ref, gb_ref, v_ref,
                 gate_ref, norm_ref, r_out, acc):
    c = HG_CHUNK
    nc = q_ref.shape[0] // c
    half = nc // 2

    def both(i, st_f, st_b):
        rows_f = pl.ds(pl.multiple_of(i * c, c), c)
        rows_b = pl.ds(pl.multiple_of((nc - 1 - i) * c, c), c)
        o_f, st_f = _hgrn_chunk(rows_f, False, mf_ref, lvf_ref, q_ref, kkf_ref, gf_ref, v_ref, st_f)
        o_b, st_b = _hgrn_chunk(rows_b, True, mb_ref, lvb_ref, q_ref, kkb_ref, gb_ref, v_ref, st_b)
        return rows_f, rows_b, o_f, o_b, st_f, st_b

    def first_touch(i, carry):
        rows_f, rows_b, o_f, o_b, st_f, st_b = both(i, *carry)
        acc[rows_f, :] = o_f
        acc[rows_b, :] = o_b
        return st_f, st_b

    def finish(rows, o):
        tot = acc[rows, :] + o
        r_out[rows, :] = (_rms(tot, norm_ref[...]) * gate_ref[rows, :].astype(F32)).astype(BF16)

    def second_touch(i, carry):
        rows_f, rows_b, o_f, o_b, st_f, st_b = both(i, *carry)
        finish(rows_f, o_f)
        finish(rows_b, o_b)
        return st_f, st_b

    zero = jnp.zeros((HG_VAL, HG_KEY), F32)
    carry = lax.fori_loop(0, half, first_touch, (zero, zero))
    lax.fori_loop(half, nc, second_touch, carry)


def _hgrn(qs, kkf, kkb, gf, gb, hv, gate, norm):
    bsz, seq, _ = qs.shape
    assert seq % (2 * HG_CHUNK) == 0
    m_f, m_b, lv_f, lv_b = _hgrn_consts()
    head = pl.BlockSpec((None, seq, HG_KEY), lambda b, h: (b, 0, h))
    return pl.pallas_call(
        _hgrn_kernel,
        grid=(bsz, HG_HEADS),
        in_specs=[_const_spec(m_f.shape), _const_spec(m_b.shape), _const_spec(lv_f.shape), _const_spec(lv_b.shape),
                  head, head, head, head, head, head, head,
                  pl.BlockSpec((1, HG_VAL), lambda b, h: (0, h))],
        out_specs=head,
        out_shape=jax.ShapeDtypeStruct((bsz, seq, HG_W), BF16),
        scratch_shapes=[pltpu.VMEM((seq, HG_VAL), F32)],
        compiler_params=pltpu.CompilerParams(dimension_semantics=("parallel", "parallel"),
                                             vmem_limit_bytes=VMEM_LIMIT),
        name="hgrn",
    )(jnp.asarray(m_f, BF16), jnp.asarray(m_b, BF16), jnp.asarray(lv_f), jnp.asarray(lv_b),
      qs, kkf, kkb, gf, gb, hv, gate, norm)


_FF_CHUNK = 1024


def _post_kernel(x_ref, at_ref, r_ref, p_ref, na_ref, wo_ref, n1_ref, n2_ref, wup_ref, wdn_ref, n3_ref,
                 wple_ref, wpg_ref, n4_ref, x_out):
    a = _rms(at_ref[...].astype(F32).T, na_ref[...]).astype(BF16)
    mix = _dot(a, wo_ref[:ATT_WIDTH, :]) + _dot(r_ref[...], wo_ref[ATT_WIDTH:, :])
    x = x_ref[...] + _rms(mix, n1_ref[...])
    h = _rms(x, n2_ref[...]).astype(BF16)
    m = jnp.zeros_like(x)
    for c0 in range(0, D_FF, _FF_CHUNK):
        act = jnp.square(jnp.maximum(_dot(h, wup_ref[:, c0:c0 + _FF_CHUNK]), 0.0)).astype(BF16)
        m = m + _dot(act, wdn_ref[c0:c0 + _FF_CHUNK, :])
    x = x + _rms(m, n3_ref[...])
    e = _dot(p_ref[...].astype(BF16), wple_ref[...])
    gate = jax.nn.sigmoid(_dot(x.astype(BF16), wpg_ref[...]))
    x_out[...] = x + _rms(e * gate, n4_ref[...])


def _post(x, at, r, p, na, wo, n1, n2, wup, wdn, n3, wple, wpg, n4, *, tm):
    n = x.shape[0]
    nt = at.shape[2] // tm
    row = lambda w: pl.BlockSpec((tm, w), lambda i: (i, 0))
    at_spec = pl.BlockSpec((None, ATT_WIDTH, tm), lambda i: (i // nt, 0, i % nt))
    vec = _const_spec((1, D_MODEL))
    single = lambda w: pl.BlockSpec(w.shape, lambda i: (0, 0), pipeline_mode=pl.Buffered(1))
    return pl.pallas_call(
        _post_kernel,
        grid=(n // tm,),
        in_specs=[row(D_MODEL), at_spec, row(HG_W), row(PLE_DIM), _const_spec((1, ATT_WIDTH)), single(wo), vec, vec,
                  single(wup), single(wdn), vec, single(wple), single(wpg), vec],
        out_specs=row(D_MODEL),
        out_shape=jax.ShapeDtypeStruct((n, D_MODEL), F32),
        compiler_params=pltpu.CompilerParams(dimension_semantics=("parallel",), vmem_limit_bytes=VMEM_LIMIT),
        name="post",
    )(x, at, r, p, na, wo, n1, n2, wup, wdn, n3, wple, wpg, n4)


def _layout_w_in(w):
    cq_ckv = w[:, :Q_LORA + KV_LORA]
    kr = w[:, Q_LORA + KV_LORA:Q_LORA + KV_LORA + ROPE_DIM]
    rest = w[:, Q_LORA + KV_LORA + ROPE_DIM:]
    half = ROPE_DIM // 2
    pad = jnp.zeros((w.shape[0], LANES - ROPE_DIM), w.dtype)
    kr_swap = jnp.concatenate([-kr[:, half:], kr[:, :half]], axis=1)
    return jnp.concatenate([cq_ckv, kr, pad, kr_swap, pad, rest], axis=1)


def _layout_w_uq(w):
    half = ROPE_DIM // 2
    w = w.reshape(Q_LORA, ATT_HEADS, QK_DIM)
    nope, x1, x2 = w[..., :NOPE_DIM], w[..., NOPE_DIM:NOPE_DIM + half], w[..., NOPE_DIM + half:]
    pad = jnp.zeros((Q_LORA, ATT_HEADS, HEAD_SLAB - QK_DIM), w.dtype)
    plain = jnp.concatenate([nope, x1, x2, pad], axis=-1)
    swap = jnp.concatenate([jnp.zeros_like(nope), -x2, x1, pad], axis=-1)
    return jnp.concatenate([plain.reshape(Q_LORA, QK_SLAB), swap.reshape(Q_LORA, QK_SLAB)], axis=1)


def _layout_w_ukv(w):
    w = w.reshape(KV_LORA, ATT_HEADS, NOPE_DIM + V_DIM)
    pad = jnp.zeros((KV_LORA, ATT_HEADS, HEAD_SLAB - NOPE_DIM), w.dtype)
    k = jnp.concatenate([w[..., :NOPE_DIM], pad], axis=-1).reshape(KV_LORA, QK_SLAB)
    vpad = jnp.zeros((KV_LORA, ATT_HEADS, V_ROWS - V_DIM), w.dtype)
    v = jnp.concatenate([w[..., NOPE_DIM:], vpad], axis=-1).reshape(KV_LORA, VT_ROWS)
    return jnp.concatenate([k, v], axis=1)


def _value_ones():
    o = np.zeros((ATT_HEADS, V_ROWS), np.float32)
    o[:, V_DIM:] = 1.0
    return o.reshape(1, VT_ROWS)


def _rope_placement():
    e = np.zeros((LANES, QK_SLAB), np.float32)
    for hd in range(ATT_HEADS):
        e[np.arange(ROPE_DIM), hd * HEAD_SLAB + NOPE_DIM + np.arange(ROPE_DIM)] = 1.0
    return e


def _rope_tables(seq):
    inv = 1.0 / (ROPE_THETA ** (jnp.arange(0, ROPE_DIM, 2, dtype=F32) / ROPE_DIM))
    ang = jnp.arange(seq, dtype=F32)[:, None] * inv[None, :]
    cos, sin = jnp.cos(ang), jnp.sin(ang)
    scale = jnp.asarray(QK_DIM ** -0.5 * LOG2E, F32)
    z = lambda w: jnp.zeros((seq, w), F32)
    cq = jnp.concatenate([jnp.ones((seq, NOPE_DIM), F32), cos, cos, z(HEAD_SLAB - QK_DIM)], axis=1) * scale
    sq = jnp.concatenate([z(NOPE_DIM), sin, sin, z(HEAD_SLAB - QK_DIM)], axis=1) * scale
    ck = jnp.concatenate([cos, cos, z(LANES - ROPE_DIM)], axis=1)
    sk = jnp.concatenate([sin, sin, z(LANES - ROPE_DIM)], axis=1)
    return cq, sq, ck, sk


def _trunk(x, p, lbs, norm_mix_pre, w_in, q_norm, w_uq, kv_norm, w_ukv, att_out_norm, hg_out_norm, w_o,
           norm_mix_post, norm_mlp_pre, w_up, w_down, norm_mlp_post, w_ple, w_ple_gate, norm_ple):
    bsz, seq, _ = x.shape
    n = bsz * seq
    tm = min(512, seq)
    tq = min(1024, seq)
    cq, sq, ck, sk = _rope_tables(seq)
    e1 = jnp.asarray(_rope_placement(), BF16)
    ones = jnp.asarray(_value_ones())
    row = lambda v: v.reshape(1, -1).astype(F32)
    xf = x.reshape(n, D_MODEL)
    for i in range(DEPTH):
        q, k, vt, hq, kkf, kkb, gf, gb, hv, gate = _in_proj(
            xf, row(norm_mix_pre[i]), _layout_w_in(w_in[i]).astype(BF16), row(q_norm[i]), row(kv_norm[i]),
            _layout_w_uq(w_uq[i]).astype(BF16), _layout_w_ukv(w_ukv[i]).astype(BF16), e1, ones,
            cq, sq, ck, sk, row(lbs[0, i]), row(lbs[1, i]), seq=seq, tm=tm)
        b3 = lambda t: t.reshape(bsz, seq, t.shape[-1])
        at = _attention(b3(q), b3(k), vt, tq=tq)
        r = _hgrn(b3(hq), b3(kkf), b3(kkb), b3(gf), b3(gb), b3(hv), b3(gate), row(hg_out_norm[i]))
        xf = _post(xf, at, r.reshape(n, HG_W), p[i].reshape(n, PLE_DIM), row(att_out_norm[i]),
                   w_o[i].astype(BF16), row(norm_mix_post[i]), row(norm_mlp_pre[i]), w_up[i].astype(BF16),
                   w_down[i].astype(BF16), row(norm_mlp_post[i]), w_ple[i].astype(BF16),
                   w_ple_gate[i].astype(BF16), row(norm_ple[i]), tm=tm)
    return xf.reshape(bsz, seq, D_MODEL)


def kernel(x_prompt, x_sample, p_prompt, p_sample, norm_mix_pre, w_in, q_norm, w_uq, kv_norm, w_ukv, att_out_norm, hg_lb, hg_out_norm, w_o, norm_mix_post, norm_mlp_pre, w_up, w_down, norm_mlp_post, w_ple, w_ple_gate, norm_ple):
    lbs = jnp.cumsum(jax.nn.softmax(hg_lb.astype(F32), axis=1), axis=1)
    lbs = lbs - lbs[:, :1]
    assert x_prompt.shape[1:] == x_sample.shape[1:]
    nb = x_prompt.shape[0]
    x = jnp.concatenate([x_prompt, x_sample], axis=0)
    p = jnp.concatenate([p_prompt, p_sample], axis=1)
    y = _trunk(x, p, lbs, norm_mix_pre, w_in, q_norm, w_uq, kv_norm, w_ukv, att_out_norm, hg_out_norm, w_o,
               norm_mix_post, norm_mlp_pre, w_up, w_down, norm_mlp_post, w_ple, w_ple_gate, norm_ple)
    return (y[:nb], y[nb:])
```

```python
import functools

import numpy as np
import jax
import jax.numpy as jnp
from jax import lax
from jax.experimental import pallas as pl
from jax.experimental.pallas import tpu as pltpu

D_MODEL = 1024
DEPTH = 4
ATT_HEADS = 8
Q_LORA = 256
KV_LORA = 128
NOPE_DIM = 64
ROPE_DIM = 32
V_DIM = 64
QK_DIM = NOPE_DIM + ROPE_DIM
ATT_WIDTH = ATT_HEADS * V_DIM
HG_HEADS = 4
HG_KEY = 128
HG_VAL = 128
HG_W = HG_HEADS * HG_KEY
D_FF = 4 * D_MODEL
PLE_DIM = 256
ROPE_THETA = 10000.0
EPS = 1e-6

LANES = 128
HEAD_SLAB = LANES
QK_SLAB = ATT_HEADS * HEAD_SLAB
V_ROWS = V_DIM + 16
VT_ROWS = ATT_HEADS * V_ROWS
LOG2E = 1.4426950408889634
HG_CHUNK = 128
VMEM_LIMIT = 56 * 1024 * 1024

F32 = jnp.float32
BF16 = jnp.bfloat16

_C_CQ = 0
_C_CKV = _C_CQ + Q_LORA
_C_KR = _C_CKV + KV_LORA
_C_KRS = _C_KR + LANES
_C_HQ = _C_KRS + LANES
_C_HFF = _C_HQ + HG_W
_C_HFB = _C_HFF + HG_W
_C_HI = _C_HFB + HG_W
_C_HG = _C_HI + HG_W
_IN_COLS_P = _C_HG + HG_W


def _rms(x, gain):
    return x * lax.rsqrt(jnp.mean(x * x, axis=-1, keepdims=True) + EPS) * gain


def _dot(a, b):
    return jnp.dot(a, b, preferred_element_type=F32)


def _dot_nt(a, b):
    return lax.dot_general(a, b, (((1,), (1,)), ((), ())), preferred_element_type=F32)


def _dot_tn(a, b):
    return lax.dot_general(a, b, (((0,), (0,)), ((), ())), preferred_element_type=F32)


def _const_spec(shape):
    return pl.BlockSpec(shape, lambda *_: (0,) * len(shape))


def _in_proj_kernel(x_ref, g_ref, win_ref, qn_ref, kvn_ref, wuq_ref, wukv_ref, e1_ref, ones_ref,
                    cq_ref, sq_ref, ck_ref, sk_ref, lbf_ref, lbb_ref,
                    q_out, k_out, v_out, hq_out, kkf_out, kkb_out, gf_out, gb_out, hv_out, gate_out):
    h = _rms(x_ref[...], g_ref[...]).astype(BF16)
    z = _dot(h, win_ref[...])

    qn = _rms(z[:, _C_CQ:_C_CKV], qn_ref[...]).astype(BF16)
    q2 = _dot(qn, wuq_ref[...])
    cq = cq_ref[...]
    sq = sq_ref[...]
    for hd in range(ATT_HEADS):
        a = q2[:, hd * HEAD_SLAB:(hd + 1) * HEAD_SLAB]
        b = q2[:, QK_SLAB + hd * HEAD_SLAB:QK_SLAB + (hd + 1) * HEAD_SLAB]
        q_out[:, hd * HEAD_SLAB:(hd + 1) * HEAD_SLAB] = (a * cq + b * sq).astype(BF16)

    kvn = _rms(z[:, _C_CKV:_C_KR], kvn_ref[...]).astype(BF16)
    kv2 = _dot(kvn, wukv_ref[...])
    kr = z[:, _C_KR:_C_KRS] * ck_ref[...] + z[:, _C_KRS:_C_HQ] * sk_ref[...]
    krp = _dot(kr.astype(BF16), e1_ref[...])
    k_out[...] = (kv2[:, :QK_SLAB] + krp).astype(BF16)
    v_out[...] = (kv2[:, QK_SLAB:] + ones_ref[...]).T.astype(BF16)

    hq = z[:, _C_HQ:_C_HFF]
    hq_out[...] = (hq * jax.nn.sigmoid(hq)).astype(BF16)
    for src, lb_ref, kk_o, g_o in ((_C_HFF, lbf_ref, kkf_out, gf_out), (_C_HFB, lbb_ref, kkb_out, gb_out)):
        lb = lb_ref[...]
        f = lb + (1.0 - lb) * jax.nn.sigmoid(z[:, src:src + HG_W])
        g_o[...] = jnp.log2(f)
        kk_o[...] = (1.0 - f).astype(BF16)
    hv_out[...] = z[:, _C_HI:_C_HG].astype(BF16)
    hg = z[:, _C_HG:_IN_COLS_P]
    gate_out[...] = (hg * jax.nn.sigmoid(hg)).astype(BF16)


def _in_proj(x, gain, win, qn, kvn, wuq, wukv, e1, ones, cq, sq, ck, sk, lbf, lbb, *, seq, tm):
    n = x.shape[0]
    nt = seq // tm
    row = lambda w: pl.BlockSpec((tm, w), lambda i: (i, 0))
    tab = pl.BlockSpec((tm, LANES), lambda i: (i % nt, 0))
    outs = [(QK_SLAB, BF16), (QK_SLAB, BF16), None, (HG_W, BF16), (HG_W, BF16), (HG_W, BF16),
            (HG_W, F32), (HG_W, F32), (HG_W, BF16), (HG_W, BF16)]
    vt_spec = pl.BlockSpec((None, VT_ROWS, tm), lambda i: (i, 0, 0))
    vt_shape = jax.ShapeDtypeStruct((n // tm, VT_ROWS, tm), BF16)
    return pl.pallas_call(
        _in_proj_kernel,
        grid=(n // tm,),
        in_specs=[row(D_MODEL), _const_spec((1, D_MODEL)), _const_spec(win.shape),
                  _const_spec((1, Q_LORA)), _const_spec((1, KV_LORA)), _const_spec(wuq.shape),
                  _const_spec(wukv.shape), _const_spec(e1.shape), _const_spec(ones.shape), tab, tab, tab, tab,
                  _const_spec((1, HG_W)), _const_spec((1, HG_W))],
        out_specs=[vt_spec if o is None else row(o[0]) for o in outs],
        out_shape=[vt_shape if o is None else jax.ShapeDtypeStruct((n, o[0]), o[1]) for o in outs],
        compiler_params=pltpu.CompilerParams(dimension_semantics=("parallel",), vmem_limit_bytes=VMEM_LIMIT),
        name="in_proj",
    )(x, gain, win, qn, kvn, wuq, wukv, e1, ones, cq, sq, ck, sk, lbf, lbb)


def _attention_kernel(q_ref, k_ref, vt_ref, ot_out, st_scr, acc_scr):
    nkb, _, kb = vt_ref.shape
    q = q_ref[...]
    st_scr[0] = _dot_nt(k_ref[0:kb, :], q)
    m = jnp.full((1, q.shape[0]), -jnp.inf, F32)
    for j in range(nkb):
        if j + 1 < nkb:
            st_scr[(j + 1) % 2] = _dot_nt(k_ref[(j + 1) * kb:(j + 2) * kb, :], q)
        st = st_scr[j % 2]
        m_new = jnp.maximum(m, jnp.max(st, axis=0, keepdims=True))
        p = jnp.exp2(st - m_new).astype(BF16)
        pv = _dot(vt_ref[j], p)
        acc_scr[...] = pv if j == 0 else jnp.exp2(m - m_new) * acc_scr[...] + pv
        m = m_new
    acc = acc_scr[...]
    ot_out[...] = (acc[:V_DIM] / acc[V_DIM:V_DIM + 1]).astype(BF16)


def _attention(q, k, vt, *, tq):
    bsz, seq, _ = q.shape
    nkb = vt.shape[0] // bsz
    kb = vt.shape[2]
    return pl.pallas_call(
        _attention_kernel,
        grid=(bsz, ATT_HEADS, seq // tq),
        in_specs=[pl.BlockSpec((None, tq, HEAD_SLAB), lambda b, h, i: (b, i, h)),
                  pl.BlockSpec((None, seq, HEAD_SLAB), lambda b, h, i: (b, 0, h)),
                  pl.BlockSpec((nkb, V_ROWS, kb), lambda b, h, i: (b, h, 0))],
        out_specs=pl.BlockSpec((None, V_DIM, tq), lambda b, h, i: (b, h, i)),
        out_shape=jax.ShapeDtypeStruct((bsz, ATT_WIDTH, seq), BF16),
        scratch_shapes=[pltpu.VMEM((2, kb, tq), F32), pltpu.VMEM((V_ROWS, tq), F32)],
        compiler_params=pltpu.CompilerParams(dimension_semantics=("parallel", "parallel", "arbitrary"),
                                             vmem_limit_bytes=VMEM_LIMIT),
        name="attention",
    )(q, k, vt)


_HG_LEVELS = int(np.log2(HG_CHUNK))
_HG_DIAG = _HG_LEVELS
_HG_SUBLANES = 8


@functools.lru_cache(maxsize=None)
def _hgrn_consts():
    c = HG_CHUNK
    t = np.arange(c)[:, None]
    u = np.arange(c)[None, :]
    blocks = [u <= t]
    level = np.full((c, c), -1, np.int32)
    level[np.arange(c), np.arange(c)] = _HG_DIAG
    for li in range(_HG_LEVELS):
        h = (c // 2) >> li
        mid = (t // (2 * h)) * (2 * h) + h - 1
        upper = (t % (2 * h)) >= h
        if h < _HG_SUBLANES:
            blocks.append((upper & (u > mid) & (u <= t)) | (~upper & (u > t) & (u <= mid)))
        same = (t // (2 * h)) == (u // (2 * h))
        level[same & upper & ((u % (2 * h)) < h)] = li
    m_f = np.concatenate(blocks, 0).astype(np.float32)
    m_b = np.concatenate([blk[::-1, ::-1] for blk in blocks], 0).astype(np.float32)
    return m_f, m_b, level, np.ascontiguousarray(level.T)


def _hgrn_chunk(rows, backward, m_ref, lv_ref, q_ref, kk_ref, g_ref, v_ref, st):
    c = HG_CHUNK
    g = g_ref[rows, :]
    g_hi = g.astype(BF16)
    g_lo = (g - g_hi.astype(F32)).astype(BF16)
    r = _dot(m_ref[...], jnp.concatenate([g_hi, g_lo], axis=1))
    x = r[:, :HG_KEY] + r[:, HG_KEY:]
    b = x[0:c]
    q = q_ref[rows, :].astype(F32)
    kk = kk_ref[rows, :].astype(F32)
    v = v_ref[rows, :]
    lv = lv_ref[...]
    tbit = lax.broadcasted_iota(jnp.int32, (c, HG_KEY), 0)

    tile = _HG_SUBLANES
    lvr = [lv[r * tile:(r + 1) * tile] for r in range(c // tile)]
    diag = _dot_nt(q.astype(BF16), kk.astype(BF16))
    srows = [jnp.where(lvr[r] == _HG_DIAG, diag[r * tile:(r + 1) * tile], 0.0) for r in range(c // tile)]
    small = 1
    for li in range(_HG_LEVELS):
        h = (c // 2) >> li
        if h >= tile:
            parts, qparts, qrows = [], [], []
            for s0 in range(0, c, 2 * h):
                lo, hi = slice(s0, s0 + h), slice(s0 + h, s0 + 2 * h)
                if backward:
                    bm = b[s0 + h:s0 + h + 1]
                    pq, pk = q[lo] * jnp.exp2(b[lo] - bm), kk[hi] * jnp.exp2(bm - b[hi])
                    parts += [pq, pk]
                    qrows += range(s0 // tile, (s0 + h) // tile)
                else:
                    bm = b[s0 + h - 1:s0 + h]
                    pk, pq = kk[lo] * jnp.exp2(bm - b[lo]), q[hi] * jnp.exp2(b[hi] - bm)
                    parts += [pk, pq]
                    qrows += range((s0 + h) // tile, (s0 + 2 * h) // tile)
                qparts.append(pq)
            gram = _dot_nt(jnp.concatenate(qparts, axis=0).astype(BF16),
                           jnp.concatenate(parts, axis=0).astype(BF16))
        else:
            is_query = ((tbit & h) == 0) if backward else ((tbit & h) != 0)
            p = (jnp.where(is_query, q, kk) * jnp.exp2(x[small * c:(small + 1) * c])).astype(BF16)
            small += 1
            gram = _dot_nt(p, p)
            qrows = range(c // tile)
        for i, r in enumerate(qrows):
            srows[r] = jnp.where(lvr[r] == li, gram[i * tile:(i + 1) * tile], srows[r])
    scores = jnp.concatenate(srows, axis=0)

    b_end = b[0:1] if backward else b[c - 1:c]
    o = _dot(scores.astype(BF16), v)
    o = o + _dot_nt((q * jnp.exp2(b)).astype(BF16), st.astype(BF16))
    kd = (kk * jnp.exp2(b_end - b)).astype(BF16)
    st_new = st * jnp.exp2(b_end) + _dot_tn(v, kd)
    return o, st_new


def _hgrn_kernel(mf_ref, mb_ref, lvf_ref, lvb_ref, q_ref, kkf_ref, kkb_ref, gf_ref, gb_ref, v_ref,
                 gate_ref, norm_ref, r_out, acc):
    c = HG_CHUNK
    nc = q_ref.shape[0] // c
    half = nc // 2

    def both(i, st_f, st_b):
        rows_f = pl.ds(pl.multiple_of(i * c, c), c)
        rows_b = pl.ds(pl.multiple_of((nc - 1 - i) * c, c), c)
        o_f, st_f = _hgrn_chunk(rows_f, False, mf_ref, lvf_ref, q_ref, kkf_ref, gf_ref, v_ref, st_f)
        o_b, st_b = _hgrn_chunk(rows_b, True, mb_ref, lvb_ref, q_ref, kkb_ref, gb_ref, v_ref, st_b)
        return rows_f, rows_b, o_f, o_b, st_f, st_b

    def first_touch(i, carry):
        rows_f, rows_b, o_f, o_b, st_f, st_b = both(i, *carry)
        acc[rows_f, :] = o_f
        acc[rows_b, :] = o_b
        return st_f, st_b

    def finish(rows, o):
        tot = acc[rows, :] + o
        r_out[rows, :] = (_rms(tot, norm_ref[...]) * gate_ref[rows, :].astype(F32)).astype(BF16)

    def second_touch(i, carry):
        rows_f, rows_b, o_f, o_b, st_f, st_b = both(i, *carry)
        finish(rows_f, o_f)
        finish(rows_b, o_b)
        return st_f, st_b

    zero = jnp.zeros((HG_VAL, HG_KEY), F32)
    carry = lax.fori_loop(0, half, first_touch, (zero, zero), unroll=4)
    lax.fori_loop(half, nc, second_touch, carry, unroll=4)


def _hgrn(qs, kkf, kkb, gf, gb, hv, gate, norm):
    bsz, seq, _ = qs.shape
    assert seq % (8 * HG_CHUNK) == 0
    m_f, m_b, lv_f, lv_b = _hgrn_consts()
    head = pl.BlockSpec((None, seq, HG_KEY), lambda b, h: (b, 0, h))
    return pl.pallas_call(
        _hgrn_kernel,
        grid=(bsz, HG_HEADS),
        in_specs=[_const_spec(m_f.shape), _const_spec(m_b.shape), _const_spec(lv_f.shape), _const_spec(lv_b.shape),
                  head, head, head, head, head, head, head,
                  pl.BlockSpec((1, HG_VAL), lambda b, h: (0, h))],
        out_specs=head,
        out_shape=jax.ShapeDtypeStruct((bsz, seq, HG_W), BF16),
        scratch_shapes=[pltpu.VMEM((seq, HG_VAL), F32)],
        compiler_params=pltpu.CompilerParams(dimension_semantics=("parallel", "parallel"),
                                             vmem_limit_bytes=VMEM_LIMIT),
        name="hgrn",
    )(jnp.asarray(m_f, BF16), jnp.asarray(m_b, BF16), jnp.asarray(lv_f), jnp.asarray(lv_b),
      qs, kkf, kkb, gf, gb, hv, gate, norm)


_FF_CHUNK = 1024


def _post_kernel(x_ref, at_ref, r_ref, p_ref, na_ref, wo_ref, n1_ref, n2_ref, wup_ref, wdn_ref, n3_ref,
                 wple_ref, wpg_ref, n4_ref, x_out):
    a = _rms(at_ref[...].astype(F32).T, na_ref[...]).astype(BF16)
    mix = _dot(a, wo_ref[:ATT_WIDTH, :]) + _dot(r_ref[...], wo_ref[ATT_WIDTH:, :])
    x = x_ref[...] + _rms(mix, n1_ref[...])
    h = _rms(x, n2_ref[...]).astype(BF16)
    m = jnp.zeros_like(x)
    for c0 in range(0, D_FF, _FF_CHUNK):
        act = jnp.square(jnp.maximum(_dot(h, wup_ref[:, c0:c0 + _FF_CHUNK]), 0.0)).astype(BF16)
        m = m + _dot(act, wdn_ref[c0:c0 + _FF_CHUNK, :])
    x = x + _rms(m, n3_ref[...])
    e = _dot(p_ref[...].astype(BF16), wple_ref[...])
    gate = jax.nn.sigmoid(_dot(x.astype(BF16), wpg_ref[...]))
    x_out[...] = x + _rms(e * gate, n4_ref[...])


def _post(x, at, r, p, na, wo, n1, n2, wup, wdn, n3, wple, wpg, n4, *, tm):
    n = x.shape[0]
    nt = at.shape[2] // tm
    row = lambda w: pl.BlockSpec((tm, w), lambda i: (i, 0))
    at_spec = pl.BlockSpec((None, ATT_WIDTH, tm), lambda i: (i // nt, 0, i % nt))
    vec = _const_spec((1, D_MODEL))
    single = lambda w: pl.BlockSpec(w.shape, lambda i: (0, 0), pipeline_mode=pl.Buffered(1))
    return pl.pallas_call(
        _post_kernel,
        grid=(n // tm,),
        in_specs=[row(D_MODEL), at_spec, row(HG_W), row(PLE_DIM), _const_spec((1, ATT_WIDTH)), single(wo), vec, vec,
                  single(wup), single(wdn), vec, single(wple), single(wpg), vec],
        out_specs=row(D_MODEL),
        out_shape=jax.ShapeDtypeStruct((n, D_MODEL), F32),
        compiler_params=pltpu.CompilerParams(dimension_semantics=("parallel",), vmem_limit_bytes=VMEM_LIMIT),
        name="post",
    )(x, at, r, p, na, wo, n1, n2, wup, wdn, n3, wple, wpg, n4)


def _layout_w_in(w):
    cq_ckv = w[:, :Q_LORA + KV_LORA]
    kr = w[:, Q_LORA + KV_LORA:Q_LORA + KV_LORA + ROPE_DIM]
    rest = w[:, Q_LORA + KV_LORA + ROPE_DIM:]
    half = ROPE_DIM // 2
    pad = jnp.zeros((w.shape[0], LANES - ROPE_DIM), w.dtype)
    kr_swap = jnp.concatenate([-kr[:, half:], kr[:, :half]], axis=1)
    return jnp.concatenate([cq_ckv, kr, pad, kr_swap, pad, rest], axis=1)


def _layout_w_uq(w):
    half = ROPE_DIM // 2
    w = w.reshape(Q_LORA, ATT_HEADS, QK_DIM)
    nope, x1, x2 = w[..., :NOPE_DIM], w[..., NOPE_DIM:NOPE_DIM + half], w[..., NOPE_DIM + half:]
    pad = jnp.zeros((Q_LORA, ATT_HEADS, HEAD_SLAB - QK_DIM), w.dtype)
    plain = jnp.concatenate([nope, x1, x2, pad], axis=-1)
    swap = jnp.concatenate([jnp.zeros_like(nope), -x2, x1, pad], axis=-1)
    return jnp.concatenate([plain.reshape(Q_LORA, QK_SLAB), swap.reshape(Q_LORA, QK_SLAB)], axis=1)


def _layout_w_ukv(w):
    w = w.reshape(KV_LORA, ATT_HEADS, NOPE_DIM + V_DIM)
    pad = jnp.zeros((KV_LORA, ATT_HEADS, HEAD_SLAB - NOPE_DIM), w.dtype)
    k = jnp.concatenate([w[..., :NOPE_DIM], pad], axis=-1).reshape(KV_LORA, QK_SLAB)
    vpad = jnp.zeros((KV_LORA, ATT_HEADS, V_ROWS - V_DIM), w.dtype)
    v = jnp.concatenate([w[..., NOPE_DIM:], vpad], axis=-1).reshape(KV_LORA, VT_ROWS)
    return jnp.concatenate([k, v], axis=1)


def _value_ones():
    o = np.zeros((ATT_HEADS, V_ROWS), np.float32)
    o[:, V_DIM:] = 1.0
    return o.reshape(1, VT_ROWS)


def _rope_placement():
    e = np.zeros((LANES, QK_SLAB), np.float32)
    for hd in range(ATT_HEADS):
        e[np.arange(ROPE_DIM), hd * HEAD_SLAB + NOPE_DIM + np.arange(ROPE_DIM)] = 1.0
    return e


def _rope_tables(seq):
    inv = 1.0 / (ROPE_THETA ** (jnp.arange(0, ROPE_DIM, 2, dtype=F32) / ROPE_DIM))
    ang = jnp.arange(seq, dtype=F32)[:, None] * inv[None, :]
    cos, sin = jnp.cos(ang), jnp.sin(ang)
    scale = jnp.asarray(QK_DIM ** -0.5 * LOG2E, F32)
    z = lambda w: jnp.zeros((seq, w), F32)
    cq = jnp.concatenate([jnp.ones((seq, NOPE_DIM), F32), cos, cos, z(HEAD_SLAB - QK_DIM)], axis=1) * scale
    sq = jnp.concatenate([z(NOPE_DIM), sin, sin, z(HEAD_SLAB - QK_DIM)], axis=1) * scale
    ck = jnp.concatenate([cos, cos, z(LANES - ROPE_DIM)], axis=1)
    sk = jnp.concatenate([sin, sin, z(LANES - ROPE_DIM)], axis=1)
    return cq, sq, ck, sk


def _trunk(x, p, lbs, norm_mix_pre, w_in, q_norm, w_uq, kv_norm, w_ukv, att_out_norm, hg_out_norm, w_o,
           norm_mix_post, norm_mlp_pre, w_up, w_down, norm_mlp_post, w_ple, w_ple_gate, norm_ple):
    bsz, seq, _ = x.shape
    n = bsz * seq
    tm = min(512, seq)
    tq = min(1024, seq)
    cq, sq, ck, sk = _rope_tables(seq)
    e1 = jnp.asarray(_rope_placement(), BF16)
    ones = jnp.asarray(_value_ones())
    row = lambda v: v.reshape(1, -1).astype(F32)
    xf = x.reshape(n, D_MODEL)
    for i in range(DEPTH):
        q, k, vt, hq, kkf, kkb, gf, gb, hv, gate = _in_proj(
            xf, row(norm_mix_pre[i]), _layout_w_in(w_in[i]).astype(BF16), row(q_norm[i]), row(kv_norm[i]),
            _layout_w_uq(w_uq[i]).astype(BF16), _layout_w_ukv(w_ukv[i]).astype(BF16), e1, ones,
            cq, sq, ck, sk, row(lbs[0, i]), row(lbs[1, i]), seq=seq, tm=tm)
        b3 = lambda t: t.reshape(bsz, seq, t.shape[-1])
        at = _attention(b3(q), b3(k), vt, tq=tq)
        r = _hgrn(b3(hq), b3(kkf), b3(kkb), b3(gf), b3(gb), b3(hv), b3(gate), row(hg_out_norm[i]))
        xf = _post(xf, at, r.reshape(n, HG_W), p[i].reshape(n, PLE_DIM), row(att_out_norm[i]),
                   w_o[i].astype(BF16), row(norm_mix_post[i]), row(norm_mlp_pre[i]), w_up[i].astype(BF16),
                   w_down[i].astype(BF16), row(norm_mlp_post[i]), w_ple[i].astype(BF16),
                   w_ple_gate[i].astype(BF16), row(norm_ple[i]), tm=tm)
    return xf.reshape(bsz, seq, D_MODEL)


def kernel(x_prompt, x_sample, p_prompt, p_sample, norm_mix_pre, w_in, q_norm, w_uq, kv_norm, w_ukv, att_out_norm, hg_lb, hg_out_norm, w_o, norm_mix_post, norm_mlp_pre, w_up, w_down, norm_mlp_post, w_ple, w_ple_gate, norm_ple):
    lbs = jnp.cumsum(jax.nn.softmax(hg_lb.astype(F32), axis=1), axis=1)
    lbs = lbs - lbs[:, :1]
    ws = (norm_mix_pre, w_in, q_norm, w_uq, kv_norm, w_ukv, att_out_norm, hg_out_norm, w_o, norm_mix_post,
          norm_mlp_pre, w_up, w_down, norm_mlp_post, w_ple, w_ple_gate, norm_ple)
    return (_trunk(x_prompt, p_prompt, lbs, *ws), _trunk(x_sample, p_sample, lbs, *ws))
```

```python
import functools

import numpy as np
import jax
import jax.numpy as jnp
from jax import lax
from jax.experimental import pallas as pl
from jax.experimental.pallas import tpu as pltpu

D_MODEL = 1024
DEPTH = 4
ATT_HEADS = 8
Q_LORA = 256
KV_LORA = 128
NOPE_DIM = 64
ROPE_DIM = 32
V_DIM = 64
QK_DIM = NOPE_DIM + ROPE_DIM
ATT_WIDTH = ATT_HEADS * V_DIM
HG_HEADS = 4
HG_KEY = 128
HG_VAL = 128
HG_W = HG_HEADS * HG_KEY
D_FF = 4 * D_MODEL
PLE_DIM = 256
ROPE_THETA = 10000.0
EPS = 1e-6

LANES = 128
HEAD_SLAB = LANES
QK_SLAB = ATT_HEADS * HEAD_SLAB
V_ROWS = V_DIM + 16
VT_ROWS = ATT_HEADS * V_ROWS
LOG2E = 1.4426950408889634
HG_CHUNK = 128
VMEM_LIMIT = 56 * 1024 * 1024

F32 = jnp.float32
BF16 = jnp.bfloat16

_C_CQ = 0
_C_CKV = _C_CQ + Q_LORA
_C_KR = _C_CKV + KV_LORA
_C_KRS = _C_KR + LANES
_C_HQ = _C_KRS + LANES
_C_HFF = _C_HQ + HG_W
_C_HFB = _C_HFF + HG_W
_C_HI = _C_HFB + HG_W
_C_HG = _C_HI + HG_W
_IN_COLS_P = _C_HG + HG_W


def _rms(x, gain):
    return x * lax.rsqrt(jnp.mean(x * x, axis=-1, keepdims=True) + EPS) * gain


def _dot(a, b):
    return jnp.dot(a, b, preferred_element_type=F32)


def _dot_nt(a, b):
    return lax.dot_general(a, b, (((1,), (1,)), ((), ())), preferred_element_type=F32)


def _dot_tn(a, b):
    return lax.dot_general(a, b, (((0,), (0,)), ((), ())), preferred_element_type=F32)


def _const_spec(shape):
    return pl.BlockSpec(shape, lambda *_: (0,) * len(shape))


def _in_proj_kernel(x_ref, g_ref, win_ref, qn_ref, kvn_ref, wuq_ref, wukv_ref, e1_ref, ones_ref,
                    cq_ref, sq_ref, ck_ref, sk_ref, lbf_ref, lbb_ref,
                    q_out, k_out, v_out, hq_out, kkf_out, kkb_out, gf_out, gb_out, hv_out, gate_out):
    h = _rms(x_ref[...], g_ref[...]).astype(BF16)
    z = _dot(h, win_ref[...])

    qn = _rms(z[:, _C_CQ:_C_CKV], qn_ref[...]).astype(BF16)
    q2 = _dot(qn, wuq_ref[...])
    cq = cq_ref[...]
    sq = sq_ref[...]
    for hd in range(ATT_HEADS):
        a = q2[:, hd * HEAD_SLAB:(hd + 1) * HEAD_SLAB]
        b = q2[:, QK_SLAB + hd * HEAD_SLAB:QK_SLAB + (hd + 1) * HEAD_SLAB]
        q_out[:, hd * HEAD_SLAB:(hd + 1) * HEAD_SLAB] = (a * cq + b * sq).astype(BF16)

    kvn = _rms(z[:, _C_CKV:_C_KR], kvn_ref[...]).astype(BF16)
    kv2 = _dot(kvn, wukv_ref[...])
    kr = z[:, _C_KR:_C_KRS] * ck_ref[...] + z[:, _C_KRS:_C_HQ] * sk_ref[...]
    krp = _dot(kr.astype(BF16), e1_ref[...])
    k_out[...] = (kv2[:, :QK_SLAB] + krp).astype(BF16)
    v_out[...] = (kv2[:, QK_SLAB:] + ones_ref[...]).T.astype(BF16)

    hq = z[:, _C_HQ:_C_HFF]
    hq_out[...] = (hq * jax.nn.sigmoid(hq)).astype(BF16)
    for src, lb_ref, kk_o, g_o in ((_C_HFF, lbf_ref, kkf_out, gf_out), (_C_HFB, lbb_ref, kkb_out, gb_out)):
        lb = lb_ref[...]
        f = lb + (1.0 - lb) * jax.nn.sigmoid(z[:, src:src + HG_W])
        g_o[...] = jnp.log2(f)
        kk_o[...] = (1.0 - f).astype(BF16)
    hv_out[...] = z[:, _C_HI:_C_HG].astype(BF16)
    hg = z[:, _C_HG:_IN_COLS_P]
    gate_out[...] = (hg * jax.nn.sigmoid(hg)).astype(BF16)


def _in_proj(x, gain, win, qn, kvn, wuq, wukv, e1, ones, cq, sq, ck, sk, lbf, lbb, *, seq, tm):
    n = x.shape[0]
    nt = seq // tm
    row = lambda w: pl.BlockSpec((tm, w), lambda i: (i, 0))
    tab = pl.BlockSpec((tm, LANES), lambda i: (i % nt, 0))
    outs = [(QK_SLAB, BF16), (QK_SLAB, BF16), None, (HG_W, BF16), (HG_W, BF16), (HG_W, BF16),
            (HG_W, F32), (HG_W, F32), (HG_W, BF16), (HG_W, BF16)]
    vt_spec = pl.BlockSpec((None, VT_ROWS, tm), lambda i: (i, 0, 0))
    vt_shape = jax.ShapeDtypeStruct((n // tm, VT_ROWS, tm), BF16)
    return pl.pallas_call(
        _in_proj_kernel,
        grid=(n // tm,),
        in_specs=[row(D_MODEL), _const_spec((1, D_MODEL)), _const_spec(win.shape),
                  _const_spec((1, Q_LORA)), _const_spec((1, KV_LORA)), _const_spec(wuq.shape),
                  _const_spec(wukv.shape), _const_spec(e1.shape), _const_spec(ones.shape), tab, tab, tab, tab,
                  _const_spec((1, HG_W)), _const_spec((1, HG_W))],
        out_specs=[vt_spec if o is None else row(o[0]) for o in outs],
        out_shape=[vt_shape if o is None else jax.ShapeDtypeStruct((n, o[0]), o[1]) for o in outs],
        compiler_params=pltpu.CompilerParams(dimension_semantics=("parallel",), vmem_limit_bytes=VMEM_LIMIT),
        name="in_proj",
    )(x, gain, win, qn, kvn, wuq, wukv, e1, ones, cq, sq, ck, sk, lbf, lbb)


def _attention_kernel(q_ref, k_ref, vt_ref, ot_out, st_scr, acc_scr):
    nkb, _, kb = vt_ref.shape
    tq = st_scr.shape[2]
    steps = [(i, j) for i in range(q_ref.shape[0] // tq) for j in range(nkb)]

    def scores(step, slot):
        i, j = step
        st_scr[slot] = _dot_nt(k_ref[j * kb:(j + 1) * kb, :], q_ref[i * tq:(i + 1) * tq, :])

    scores(steps[0], 0)
    m = None
    for n, (i, j) in enumerate(steps):
        if n + 1 < len(steps):
            scores(steps[n + 1], (n + 1) % 2)
        st = st_scr[n % 2]
        bmax = jnp.max(st, axis=0, keepdims=True)
        m_new = bmax if j == 0 else jnp.maximum(m, bmax)
        p = jnp.exp2(st - m_new).astype(BF16)
        pv = _dot(vt_ref[j], p)
        acc_scr[...] = pv if j == 0 else jnp.exp2(m - m_new) * acc_scr[...] + pv
        m = m_new
        if j == nkb - 1:
            acc = acc_scr[...]
            ot_out[:, i * tq:(i + 1) * tq] = (acc[:V_DIM] / acc[V_DIM:V_DIM + 1]).astype(BF16)


def _attention(q, k, vt, *, tq):
    bsz, seq, _ = q.shape
    nkb = vt.shape[0] // bsz
    kb = vt.shape[2]
    head = pl.BlockSpec((None, seq, HEAD_SLAB), lambda b, h: (b, 0, h))
    return pl.pallas_call(
        _attention_kernel,
        grid=(bsz, ATT_HEADS),
        in_specs=[head, head, pl.BlockSpec((nkb, V_ROWS, kb), lambda b, h: (b, h, 0))],
        out_specs=pl.BlockSpec((None, V_DIM, seq), lambda b, h: (b, h, 0)),
        out_shape=jax.ShapeDtypeStruct((bsz, ATT_WIDTH, seq), BF16),
        scratch_shapes=[pltpu.VMEM((2, kb, tq), F32), pltpu.VMEM((V_ROWS, tq), F32)],
        compiler_params=pltpu.CompilerParams(dimension_semantics=("parallel", "parallel"),
                                             vmem_limit_bytes=VMEM_LIMIT),
        name="attention",
    )(q, k, vt)


_HG_LEVELS = int(np.log2(HG_CHUNK))
_HG_DIAG = _HG_LEVELS
_HG_SUBLANES = 8


@functools.lru_cache(maxsize=None)
def _hgrn_consts():
    c = HG_CHUNK
    t = np.arange(c)[:, None]
    u = np.arange(c)[None, :]
    blocks = [u <= t]
    level = np.full((c, c), -1, np.int32)
    level[np.arange(c), np.arange(c)] = _HG_DIAG
    for li in range(_HG_LEVELS):
        h = (c // 2) >> li
        mid = (t // (2 * h)) * (2 * h) + h - 1
        upper = (t % (2 * h)) >= h
        if 1 < h < _HG_SUBLANES:
            blocks.append((upper & (u > mid) & (u <= t)) | (~upper & (u > t) & (u <= mid)))
        same = (t // (2 * h)) == (u // (2 * h))
        level[same & upper & ((u % (2 * h)) < h)] = li
    m_f = np.concatenate(blocks, 0).astype(np.float32)
    m_b = np.concatenate([blk[::-1, ::-1] for blk in blocks], 0).astype(np.float32)
    return m_f, m_b, level, np.ascontiguousarray(level.T)


def _hgrn_chunk(rows, backward, m_ref, lv_ref, q_ref, kk_ref, g_ref, v_ref, st):
    c = HG_CHUNK
    g = g_ref[rows, :]
    g_hi = g.astype(BF16)
    g_lo = (g - g_hi.astype(F32)).astype(BF16)
    r = _dot(m_ref[...], jnp.concatenate([g_hi, g_lo], axis=1))
    x = r[:, :HG_KEY] + r[:, HG_KEY:]
    b = x[0:c]
    q = q_ref[rows, :].astype(F32)
    kk = kk_ref[rows, :].astype(F32)
    v = v_ref[rows, :]
    lv = lv_ref[...]
    tbit = lax.broadcasted_iota(jnp.int32, (c, HG_KEY), 0)

    tile = _HG_SUBLANES
    lvr = [lv[r * tile:(r + 1) * tile] for r in range(c // tile)]
    diag = jnp.sum(q * kk, axis=-1, keepdims=True)
    srows = [jnp.where(lvr[r] == _HG_DIAG, diag[r * tile:(r + 1) * tile], 0.0) for r in range(c // tile)]
    kk_prev = pltpu.roll(kk, (c - 1) if backward else 1, axis=0)
    near = jnp.sum(q * kk_prev * jnp.exp2(g), axis=-1, keepdims=True)
    small = 1
    for li in range(_HG_LEVELS):
        h = (c // 2) >> li
        if h == 1:
            for r in range(c // tile):
                srows[r] = jnp.where(lvr[r] == li, near[r * tile:(r + 1) * tile], srows[r])
            continue
        if h >= tile:
            parts, qparts, qrows = [], [], []
            for s0 in range(0, c, 2 * h):
                lo, hi = slice(s0, s0 + h), slice(s0 + h, s0 + 2 * h)
                if backward:
                    bm = b[s0 + h:s0 + h + 1]
                    pq, pk = q[lo] * jnp.exp2(b[lo] - bm), kk[hi] * jnp.exp2(bm - b[hi])
                    parts += [pq, pk]
                    qrows += range(s0 // tile, (s0 + h) // tile)
                else:
                    bm = b[s0 + h - 1:s0 + h]
                    pk, pq = kk[lo] * jnp.exp2(bm - b[lo]), q[hi] * jnp.exp2(b[hi] - bm)
                    parts += [pk, pq]
                    qrows += range((s0 + h) // tile, (s0 + 2 * h) // tile)
                qparts.append(pq)
            gram = _dot_nt(jnp.concatenate(qparts, axis=0).astype(BF16),
                           jnp.concatenate(parts, axis=0).astype(BF16))
        else:
            is_query = ((tbit & h) == 0) if backward else ((tbit & h) != 0)
            p = (jnp.where(is_query, q, kk) * jnp.exp2(x[small * c:(small + 1) * c])).astype(BF16)
            small += 1
            gram = _dot_nt(p, p)
            qrows = range(c // tile)
        for i, r in enumerate(qrows):
            srows[r] = jnp.where(lvr[r] == li, gram[i * tile:(i + 1) * tile], srows[r])
    scores = jnp.concatenate(srows, axis=0)

    b_end = b[0:1] if backward else b[c - 1:c]
    o = _dot(scores.astype(BF16), v)
    o = o + _dot_nt((q * jnp.exp2(b)).astype(BF16), st.astype(BF16))
    kd = (kk * jnp.exp2(b_end - b)).astype(BF16)
    st_new = st * jnp.exp2(b_end) + _dot_tn(v, kd)
    return o, st_new


def _hgrn_kernel(mf_ref, mb_ref, lvf_ref, lvb_ref, q_ref, kkf_ref, kkb_ref, gf_ref, gb_ref, v_ref,
                 gate_ref, norm_ref, r_out, acc):
    c = HG_CHUNK
    nc = q_ref.shape[0] // c
    half = nc // 2

    def both(i, st_f, st_b):
        rows_f = pl.ds(pl.multiple_of(i * c, c), c)
        rows_b = pl.ds(pl.multiple_of((nc - 1 - i) * c, c), c)
        o_f, st_f = _hgrn_chunk(rows_f, False, mf_ref, lvf_ref, q_ref, kkf_ref, gf_ref, v_ref, st_f)
        o_b, st_b = _hgrn_chunk(rows_b, True, mb_ref, lvb_ref, q_ref, kkb_ref, gb_ref, v_ref, st_b)
        return rows_f, rows_b, o_f, o_b, st_f, st_b

    def first_touch(i, carry):
        rows_f, rows_b, o_f, o_b, st_f, st_b = both(i, *carry)
        acc[rows_f, :] = o_f
        acc[rows_b, :] = o_b
        return st_f, st_b

    def finish(rows, o):
        tot = acc[rows, :] + o
        r_out[rows, :] = (_rms(tot, norm_ref[...]) * gate_ref[rows, :].astype(F32)).astype(BF16)

    def second_touch(i, carry):
        rows_f, rows_b, o_f, o_b, st_f, st_b = both(i, *carry)
        finish(rows_f, o_f)
        finish(rows_b, o_b)
        return st_f, st_b

    zero = jnp.zeros((HG_VAL, HG_KEY), F32)
    carry = lax.fori_loop(0, half, first_touch, (zero, zero), unroll=4)
    lax.fori_loop(half, nc, second_touch, carry, unroll=4)


def _hgrn(qs, kkf, kkb, gf, gb, hv, gate, norm):
    bsz, seq, _ = qs.shape
    assert seq % (8 * HG_CHUNK) == 0
    m_f, m_b, lv_f, lv_b = _hgrn_consts()
    head = pl.BlockSpec((None, seq, HG_KEY), lambda b, h: (b, 0, h))
    return pl.pallas_call(
        _hgrn_kernel,
        grid=(bsz, HG_HEADS),
        in_specs=[_const_spec(m_f.shape), _const_spec(m_b.shape), _const_spec(lv_f.shape), _const_spec(lv_b.shape),
                  head, head, head, head, head, head, head,
                  pl.BlockSpec((1, HG_VAL), lambda b, h: (0, h))],
        out_specs=head,
        out_shape=jax.ShapeDtypeStruct((bsz, seq, HG_W), BF16),
        scratch_shapes=[pltpu.VMEM((seq, HG_VAL), F32)],
        compiler_params=pltpu.CompilerParams(dimension_semantics=("parallel", "parallel"),
                                             vmem_limit_bytes=VMEM_LIMIT),
        name="hgrn",
    )(jnp.asarray(m_f, BF16), jnp.asarray(m_b, BF16), jnp.asarray(lv_f), jnp.asarray(lv_b),
      qs, kkf, kkb, gf, gb, hv, gate, norm)


_FF_CHUNK = 1024


def _post_kernel(x_ref, at_ref, r_ref, p_ref, na_ref, wo_ref, n1_ref, n2_ref, wup_ref, wdn_ref, n3_ref,
                 wple_ref, wpg_ref, n4_ref, x_out):
    a = _rms(at_ref[...].astype(F32).T, na_ref[...]).astype(BF16)
    mix = _dot(a, wo_ref[:ATT_WIDTH, :]) + _dot(r_ref[...], wo_ref[ATT_WIDTH:, :])
    x = x_ref[...] + _rms(mix, n1_ref[...])
    h = _rms(x, n2_ref[...]).astype(BF16)
    m = jnp.zeros_like(x)
    for c0 in range(0, D_FF, _FF_CHUNK):
        act = jnp.square(jnp.maximum(_dot(h, wup_ref[:, c0:c0 + _FF_CHUNK]), 0.0)).astype(BF16)
        m = m + _dot(act, wdn_ref[c0:c0 + _FF_CHUNK, :])
    x = x + _rms(m, n3_ref[...])
    e = _dot(p_ref[...].astype(BF16), wple_ref[...])
    gate = jax.nn.sigmoid(_dot(x.astype(BF16), wpg_ref[...]))
    x_out[...] = x + _rms(e * gate, n4_ref[...])


def _post(x, at, r, p, na, wo, n1, n2, wup, wdn, n3, wple, wpg, n4, *, tm):
    n = x.shape[0]
    nt = at.shape[2] // tm
    row = lambda w: pl.BlockSpec((tm, w), lambda i: (i, 0))
    at_spec = pl.BlockSpec((None, ATT_WIDTH, tm), lambda i: (i // nt, 0, i % nt))
    vec = _const_spec((1, D_MODEL))
    single = lambda w: pl.BlockSpec(w.shape, lambda i: (0, 0), pipeline_mode=pl.Buffered(1))
    return pl.pallas_call(
        _post_kernel,
        grid=(n // tm,),
        in_specs=[row(D_MODEL), at_spec, row(HG_W), row(PLE_DIM), _const_spec((1, ATT_WIDTH)), single(wo), vec, vec,
                  single(wup), single(wdn), vec, single(wple), single(wpg), vec],
        out_specs=row(D_MODEL),
        out_shape=jax.ShapeDtypeStruct((n, D_MODEL), F32),
        compiler_params=pltpu.CompilerParams(dimension_semantics=("parallel",), vmem_limit_bytes=VMEM_LIMIT),
        name="post",
    )(x, at, r, p, na, wo, n1, n2, wup, wdn, n3, wple, wpg, n4)


def _layout_w_in(w):
    cq_ckv = w[:, :Q_LORA + KV_LORA]
    kr = w[:, Q_LORA + KV_LORA:Q_LORA + KV_LORA + ROPE_DIM]
    rest = w[:, Q_LORA + KV_LORA + ROPE_DIM:]
    half = ROPE_DIM // 2
    pad = jnp.zeros((w.shape[0], LANES - ROPE_DIM), w.dtype)
    kr_swap = jnp.concatenate([-kr[:, half:], kr[:, :half]], axis=1)
    return jnp.concatenate([cq_ckv, kr, pad, kr_swap, pad, rest], axis=1)


def _layout_w_uq(w):
    half = ROPE_DIM // 2
    w = w.reshape(Q_LORA, ATT_HEADS, QK_DIM)
    nope, x1, x2 = w[..., :NOPE_DIM], w[..., NOPE_DIM:NOPE_DIM + half], w[..., NOPE_DIM + half:]
    pad = jnp.zeros((Q_LORA, ATT_HEADS, HEAD_SLAB - QK_DIM), w.dtype)
    plain = jnp.concatenate([nope, x1, x2, pad], axis=-1)
    swap = jnp.concatenate([jnp.zeros_like(nope), -x2, x1, pad], axis=-1)
    return jnp.concatenate([plain.reshape(Q_LORA, QK_SLAB), swap.reshape(Q_LORA, QK_SLAB)], axis=1)


def _layout_w_ukv(w):
    w = w.reshape(KV_LORA, ATT_HEADS, NOPE_DIM + V_DIM)
    pad = jnp.zeros((KV_LORA, ATT_HEADS, HEAD_SLAB - NOPE_DIM), w.dtype)
    k = jnp.concatenate([w[..., :NOPE_DIM], pad], axis=-1).reshape(KV_LORA, QK_SLAB)
    vpad = jnp.zeros((KV_LORA, ATT_HEADS, V_ROWS - V_DIM), w.dtype)
    v = jnp.concatenate([w[..., NOPE_DIM:], vpad], axis=-1).reshape(KV_LORA, VT_ROWS)
    return jnp.concatenate([k, v], axis=1)


def _value_ones():
    o = np.zeros((ATT_HEADS, V_ROWS), np.float32)
    o[:, V_DIM:] = 1.0
    return o.reshape(1, VT_ROWS)


def _rope_placement():
    e = np.zeros((LANES, QK_SLAB), np.float32)
    for hd in range(ATT_HEADS):
        e[np.arange(ROPE_DIM), hd * HEAD_SLAB + NOPE_DIM + np.arange(ROPE_DIM)] = 1.0
    return e


def _rope_tables(seq):
    inv = 1.0 / (ROPE_THETA ** (jnp.arange(0, ROPE_DIM, 2, dtype=F32) / ROPE_DIM))
    ang = jnp.arange(seq, dtype=F32)[:, None] * inv[None, :]
    cos, sin = jnp.cos(ang), jnp.sin(ang)
    scale = jnp.asarray(QK_DIM ** -0.5 * LOG2E, F32)
    z = lambda w: jnp.zeros((seq, w), F32)
    cq = jnp.concatenate([jnp.ones((seq, NOPE_DIM), F32), cos, cos, z(HEAD_SLAB - QK_DIM)], axis=1) * scale
    sq = jnp.concatenate([z(NOPE_DIM), sin, sin, z(HEAD_SLAB - QK_DIM)], axis=1) * scale
    ck = jnp.concatenate([cos, cos, z(LANES - ROPE_DIM)], axis=1)
    sk = jnp.concatenate([sin, sin, z(LANES - ROPE_DIM)], axis=1)
    return cq, sq, ck, sk


def _trunk(x, p, lbs, norm_mix_pre, w_in, q_norm, w_uq, kv_norm, w_ukv, att_out_norm, hg_out_norm, w_o,
           norm_mix_post, norm_mlp_pre, w_up, w_down, norm_mlp_post, w_ple, w_ple_gate, norm_ple):
    bsz, seq, _ = x.shape
    n = bsz * seq
    tm = min(512, seq)
    tq = min(1024, seq)
    cq, sq, ck, sk = _rope_tables(seq)
    e1 = jnp.asarray(_rope_placement(), BF16)
    ones = jnp.asarray(_value_ones())
    row = lambda v: v.reshape(1, -1).astype(F32)
    xf = x.reshape(n, D_MODEL)
    for i in range(DEPTH):
        q, k, vt, hq, kkf, kkb, gf, gb, hv, gate = _in_proj(
            xf, row(norm_mix_pre[i]), _layout_w_in(w_in[i]).astype(BF16), row(q_norm[i]), row(kv_norm[i]),
            _layout_w_uq(w_uq[i]).astype(BF16), _layout_w_ukv(w_ukv[i]).astype(BF16), e1, ones,
            cq, sq, ck, sk, row(lbs[0, i]), row(lbs[1, i]), seq=seq, tm=tm)
        b3 = lambda t: t.reshape(bsz, seq, t.shape[-1])
        at = _attention(b3(q), b3(k), vt, tq=tq)
        r = _hgrn(b3(hq), b3(kkf), b3(kkb), b3(gf), b3(gb), b3(hv), b3(gate), row(hg_out_norm[i]))
        xf = _post(xf, at, r.reshape(n, HG_W), p[i].reshape(n, PLE_DIM), row(att_out_norm[i]),
                   w_o[i].astype(BF16), row(norm_mix_post[i]), row(norm_mlp_pre[i]), w_up[i].astype(BF16),
                   w_down[i].astype(BF16), row(norm_mlp_post[i]), w_ple[i].astype(BF16),
                   w_ple_gate[i].astype(BF16), row(norm_ple[i]), tm=tm)
    return xf.reshape(bsz, seq, D_MODEL)


def kernel(x_prompt, x_sample, p_prompt, p_sample, norm_mix_pre, w_in, q_norm, w_uq, kv_norm, w_ukv, att_out_norm, hg_lb, hg_out_norm, w_o, norm_mix_post, norm_mlp_pre, w_up, w_down, norm_mlp_post, w_ple, w_ple_gate, norm_ple):
    lbs = jnp.cumsum(jax.nn.softmax(hg_lb.astype(F32), axis=1), axis=1)
    lbs = lbs - lbs[:, :1]
    ws = (norm_mix_pre, w_in, q_norm, w_uq, kv_norm, w_ukv, att_out_norm, hg_out_norm, w_o, norm_mix_post,
          norm_mlp_pre, w_up, w_down, norm_mlp_post, w_ple, w_ple_gate, norm_ple)
    return (_trunk(x_prompt, p_prompt, lbs, *ws), _trunk(x_sample, p_sample, lbs, *ws))
```

```python
import functools

import numpy as np
import jax
import jax.numpy as jnp
from jax import lax
from jax.experimental import pallas as pl
from jax.experimental.pallas import tpu as pltpu

D_MODEL = 1024
DEPTH = 4
ATT_HEADS = 8
Q_LORA = 256
KV_LORA = 128
NOPE_DIM = 64
ROPE_DIM = 32
V_DIM = 64
QK_DIM = NOPE_DIM + ROPE_DIM
ATT_WIDTH = ATT_HEADS * V_DIM
HG_HEADS = 4
HG_KEY = 128
HG_VAL = 128
HG_W = HG_HEADS * HG_KEY
D_FF = 4 * D_MODEL
PLE_DIM = 256
ROPE_THETA = 10000.0
EPS = 1e-6

LANES = 128
HEAD_SLAB = LANES
QK_SLAB = ATT_HEADS * HEAD_SLAB
V_ROWS = V_DIM + 16
VT_ROWS = ATT_HEADS * V_ROWS
LOG2E = 1.4426950408889634
HG_CHUNK = 128
VMEM_LIMIT = 56 * 1024 * 1024

F32 = jnp.float32
BF16 = jnp.bfloat16

_C_CQ = 0
_C_CKV = _C_CQ + Q_LORA
_C_KR = _C_CKV + KV_LORA
_C_KRS = _C_KR + LANES
_C_HQ = _C_KRS + LANES
_C_HFF = _C_HQ + HG_W
_C_HFB = _C_HFF + HG_W
_C_HI = _C_HFB + HG_W
_C_HG = _C_HI + HG_W
_IN_COLS_P = _C_HG + HG_W


def _rms(x, gain):
    return x * lax.rsqrt(jnp.mean(x * x, axis=-1, keepdims=True) + EPS) * gain


def _dot(a, b):
    return jnp.dot(a, b, preferred_element_type=F32)


def _dot_nt(a, b):
    return lax.dot_general(a, b, (((1,), (1,)), ((), ())), preferred_element_type=F32)


def _dot_tn(a, b):
    return lax.dot_general(a, b, (((0,), (0,)), ((), ())), preferred_element_type=F32)


def _const_spec(shape):
    return pl.BlockSpec(shape, lambda *_: (0,) * len(shape))


def _in_proj_kernel(x_ref, g_ref, win_ref, qn_ref, kvn_ref, wuq_ref, wukv_ref, e1_ref, ones_ref,
                    cq_ref, sq_ref, ck_ref, sk_ref, lbf_ref, lbb_ref,
                    q_out, k_out, v_out, hq_out, kkf_out, kkb_out, gf_out, gb_out, hv_out, gate_out):
    h = _rms(x_ref[...], g_ref[...]).astype(BF16)
    z = _dot(h, win_ref[...])

    qn = _rms(z[:, _C_CQ:_C_CKV], qn_ref[...]).astype(BF16)
    q2 = _dot(qn, wuq_ref[...])
    cq = cq_ref[...]
    sq = sq_ref[...]
    for hd in range(ATT_HEADS):
        a = q2[:, hd * HEAD_SLAB:(hd + 1) * HEAD_SLAB]
        b = q2[:, QK_SLAB + hd * HEAD_SLAB:QK_SLAB + (hd + 1) * HEAD_SLAB]
        q_out[:, hd * HEAD_SLAB:(hd + 1) * HEAD_SLAB] = (a * cq + b * sq).astype(BF16)

    kvn = _rms(z[:, _C_CKV:_C_KR], kvn_ref[...]).astype(BF16)
    kv2 = _dot(kvn, wukv_ref[...])
    kr = z[:, _C_KR:_C_KRS] * ck_ref[...] + z[:, _C_KRS:_C_HQ] * sk_ref[...]
    krp = _dot(kr.astype(BF16), e1_ref[...])
    k_out[...] = (kv2[:, :QK_SLAB] + krp).astype(BF16)
    v_out[...] = (kv2[:, QK_SLAB:] + ones_ref[...]).T.astype(BF16)

    hq = z[:, _C_HQ:_C_HFF]
    hq_out[...] = (hq * jax.nn.sigmoid(hq)).astype(BF16)
    for src, lb_ref, kk_o, g_o in ((_C_HFF, lbf_ref, kkf_out, gf_out), (_C_HFB, lbb_ref, kkb_out, gb_out)):
        lb = lb_ref[...]
        f = lb + (1.0 - lb) * jax.nn.sigmoid(z[:, src:src + HG_W])
        g_o[...] = jnp.log2(f)
        kk_o[...] = (1.0 - f).astype(BF16)
    hv_out[...] = z[:, _C_HI:_C_HG].astype(BF16)
    hg = z[:, _C_HG:_IN_COLS_P]
    gate_out[...] = (hg * jax.nn.sigmoid(hg)).astype(BF16)


def _in_proj(x, gain, win, qn, kvn, wuq, wukv, e1, ones, cq, sq, ck, sk, lbf, lbb, *, seq, tm):
    n = x.shape[0]
    nt = seq // tm
    row = lambda w: pl.BlockSpec((tm, w), lambda i: (i, 0))
    tab = pl.BlockSpec((tm, LANES), lambda i: (i % nt, 0))
    outs = [(QK_SLAB, BF16), (QK_SLAB, BF16), None, (HG_W, BF16), (HG_W, BF16), (HG_W, BF16),
            (HG_W, F32), (HG_W, F32), (HG_W, BF16), (HG_W, BF16)]
    vt_spec = pl.BlockSpec((None, VT_ROWS, tm), lambda i: (i, 0, 0))
    vt_shape = jax.ShapeDtypeStruct((n // tm, VT_ROWS, tm), BF16)
    return pl.pallas_call(
        _in_proj_kernel,
        grid=(n // tm,),
        in_specs=[row(D_MODEL), _const_spec((1, D_MODEL)), _const_spec(win.shape),
                  _const_spec((1, Q_LORA)), _const_spec((1, KV_LORA)), _const_spec(wuq.shape),
                  _const_spec(wukv.shape), _const_spec(e1.shape), _const_spec(ones.shape), tab, tab, tab, tab,
                  _const_spec((1, HG_W)), _const_spec((1, HG_W))],
        out_specs=[vt_spec if o is None else row(o[0]) for o in outs],
        out_shape=[vt_shape if o is None else jax.ShapeDtypeStruct((n, o[0]), o[1]) for o in outs],
        compiler_params=pltpu.CompilerParams(dimension_semantics=("parallel",), vmem_limit_bytes=VMEM_LIMIT),
        name="in_proj",
    )(x, gain, win, qn, kvn, wuq, wukv, e1, ones, cq, sq, ck, sk, lbf, lbb)


def _attention_kernel(q_ref, k_ref, vt_ref, ot_out, st_scr, acc_scr):
    nkb, _, kb = vt_ref.shape
    tq = st_scr.shape[2]
    steps = [(i, j) for i in range(q_ref.shape[0] // tq) for j in range(nkb)]

    def scores(step, slot):
        i, j = step
        st_scr[slot] = _dot_nt(k_ref[j * kb:(j + 1) * kb, :], q_ref[i * tq:(i + 1) * tq, :])

    slots = st_scr.shape[0]
    for n in range(slots - 1):
        scores(steps[n], n)
    m = None
    for n, (i, j) in enumerate(steps):
        if n + slots - 1 < len(steps):
            scores(steps[n + slots - 1], (n + slots - 1) % slots)
        st = st_scr[n % slots]
        bmax = jnp.max(st, axis=0, keepdims=True)
        m_new = bmax if j == 0 else jnp.maximum(m, bmax)
        p = jnp.exp2(st - m_new).astype(BF16)
        pv = _dot(vt_ref[j], p)
        acc_scr[...] = pv if j == 0 else jnp.exp2(m - m_new) * acc_scr[...] + pv
        m = m_new
        if j == nkb - 1:
            acc = acc_scr[...]
            ot_out[:, i * tq:(i + 1) * tq] = (acc[:V_DIM] / acc[V_DIM:V_DIM + 1]).astype(BF16)


def _attention(q, k, vt, *, tq):
    bsz, seq, _ = q.shape
    nkb = vt.shape[0] // bsz
    kb = vt.shape[2]
    head = pl.BlockSpec((None, seq, HEAD_SLAB), lambda b, h: (b, 0, h))
    return pl.pallas_call(
        _attention_kernel,
        grid=(bsz, ATT_HEADS),
        in_specs=[head, head, pl.BlockSpec((nkb, V_ROWS, kb), lambda b, h: (b, h, 0))],
        out_specs=pl.BlockSpec((None, V_DIM, seq), lambda b, h: (b, h, 0)),
        out_shape=jax.ShapeDtypeStruct((bsz, ATT_WIDTH, seq), BF16),
        scratch_shapes=[pltpu.VMEM((3, kb, tq), F32), pltpu.VMEM((V_ROWS, tq), F32)],
        compiler_params=pltpu.CompilerParams(dimension_semantics=("parallel", "parallel"),
                                             vmem_limit_bytes=VMEM_LIMIT),
        name="attention",
    )(q, k, vt)


_HG_LEVELS = int(np.log2(HG_CHUNK))
_HG_DIAG = _HG_LEVELS
_HG_SUBLANES = 8


@functools.lru_cache(maxsize=None)
def _hgrn_consts():
    c = HG_CHUNK
    t = np.arange(c)[:, None]
    u = np.arange(c)[None, :]
    blocks = [u <= t]
    level = np.full((c, c), -1, np.int32)
    level[np.arange(c), np.arange(c)] = _HG_DIAG
    for li in range(_HG_LEVELS):
        h = (c // 2) >> li
        mid = (t // (2 * h)) * (2 * h) + h - 1
        upper = (t % (2 * h)) >= h
        if 1 < h < _HG_SUBLANES:
            blocks.append((upper & (u > mid) & (u <= t)) | (~upper & (u > t) & (u <= mid)))
        same = (t // (2 * h)) == (u // (2 * h))
        level[same & upper & ((u % (2 * h)) < h)] = li
    m_f = np.concatenate(blocks, 0).astype(np.float32)
    m_b = np.concatenate([blk[::-1, ::-1] for blk in blocks], 0).astype(np.float32)
    return m_f, m_b, level, np.ascontiguousarray(level.T)


def _hgrn_chunk(rows, backward, m_ref, lv_ref, q_ref, kk_ref, g_ref, v_ref, st):
    c = HG_CHUNK
    g = g_ref[rows, :]
    g_hi = g.astype(BF16)
    g_lo = (g - g_hi.astype(F32)).astype(BF16)
    r = _dot(m_ref[...], jnp.concatenate([g_hi, g_lo], axis=1))
    x = r[:, :HG_KEY] + r[:, HG_KEY:]
    b = x[0:c]
    q = q_ref[rows, :].astype(F32)
    kk = kk_ref[rows, :].astype(F32)
    v = v_ref[rows, :]
    lv = lv_ref[...]
    tbit = lax.broadcasted_iota(jnp.int32, (c, HG_KEY), 0)

    tile = _HG_SUBLANES
    lvr = [lv[r * tile:(r + 1) * tile] for r in range(c // tile)]
    diag = jnp.sum(q * kk, axis=-1, keepdims=True)
    srows = [jnp.where(lvr[r] == _HG_DIAG, diag[r * tile:(r + 1) * tile], 0.0) for r in range(c // tile)]
    kk_prev = pltpu.roll(kk, (c - 1) if backward else 1, axis=0)
    near = jnp.sum(q * kk_prev * jnp.exp2(g), axis=-1, keepdims=True)
    small = 1
    for li in range(_HG_LEVELS):
        h = (c // 2) >> li
        if h == 1:
            for r in range(c // tile):
                srows[r] = jnp.where(lvr[r] == li, near[r * tile:(r + 1) * tile], srows[r])
            continue
        if h >= tile:
            parts, qparts, qrows = [], [], []
            for s0 in range(0, c, 2 * h):
                lo, hi = slice(s0, s0 + h), slice(s0 + h, s0 + 2 * h)
                if backward:
                    bm = b[s0 + h:s0 + h + 1]
                    pq, pk = q[lo] * jnp.exp2(b[lo] - bm), kk[hi] * jnp.exp2(bm - b[hi])
                    parts += [pq, pk]
                    qrows += range(s0 // tile, (s0 + h) // tile)
                else:
                    bm = b[s0 + h - 1:s0 + h]
                    pk, pq = kk[lo] * jnp.exp2(bm - b[lo]), q[hi] * jnp.exp2(b[hi] - bm)
                    parts += [pk, pq]
                    qrows += range((s0 + h) // tile, (s0 + 2 * h) // tile)
                qparts.append(pq)
            gram = _dot_nt(jnp.concatenate(qparts, axis=0).astype(BF16),
                           jnp.concatenate(parts, axis=0).astype(BF16))
        else:
            is_query = ((tbit & h) == 0) if backward else ((tbit & h) != 0)
            p = (jnp.where(is_query, q, kk) * jnp.exp2(x[small * c:(small + 1) * c])).astype(BF16)
            small += 1
            gram = _dot_nt(p, p)
            qrows = range(c // tile)
        for i, r in enumerate(qrows):
            srows[r] = jnp.where(lvr[r] == li, gram[i * tile:(i + 1) * tile], srows[r])
    scores = jnp.concatenate(srows, axis=0)

    b_end = b[0:1] if backward else b[c - 1:c]
    o = _dot(scores.astype(BF16), v)
    o = o + _dot_nt((q * jnp.exp2(b)).astype(BF16), st.astype(BF16))
    kd = (kk * jnp.exp2(b_end - b)).astype(BF16)
    st_new = st * jnp.exp2(b_end) + _dot_tn(v, kd)
    return o, st_new


def _hgrn_kernel(mf_ref, mb_ref, lvf_ref, lvb_ref, q_ref, kkf_ref, kkb_ref, gf_ref, gb_ref, v_ref,
                 gate_ref, norm_ref, r_out, acc):
    c = HG_CHUNK
    nc = q_ref.shape[0] // c
    half = nc // 2

    def both(i, st_f, st_b):
        rows_f = pl.ds(pl.multiple_of(i * c, c), c)
        rows_b = pl.ds(pl.multiple_of((nc - 1 - i) * c, c), c)
        o_f, st_f = _hgrn_chunk(rows_f, False, mf_ref, lvf_ref, q_ref, kkf_ref, gf_ref, v_ref, st_f)
        o_b, st_b = _hgrn_chunk(rows_b, True, mb_ref, lvb_ref, q_ref, kkb_ref, gb_ref, v_ref, st_b)
        return rows_f, rows_b, o_f, o_b, st_f, st_b

    def first_touch(i, carry):
        rows_f, rows_b, o_f, o_b, st_f, st_b = both(i, *carry)
        acc[rows_f, :] = o_f
        acc[rows_b, :] = o_b
        return st_f, st_b

    def finish(rows, o):
        tot = acc[rows, :] + o
        r_out[rows, :] = (_rms(tot, norm_ref[...]) * gate_ref[rows, :].astype(F32)).astype(BF16)

    def second_touch(i, carry):
        rows_f, rows_b, o_f, o_b, st_f, st_b = both(i, *carry)
        finish(rows_f, o_f)
        finish(rows_b, o_b)
        return st_f, st_b

    zero = jnp.zeros((HG_VAL, HG_KEY), F32)
    carry = lax.fori_loop(0, half, first_touch, (zero, zero), unroll=4)
    lax.fori_loop(half, nc, second_touch, carry, unroll=4)


def _hgrn(qs, kkf, kkb, gf, gb, hv, gate, norm):
    bsz, seq, _ = qs.shape
    assert seq % (8 * HG_CHUNK) == 0
    m_f, m_b, lv_f, lv_b = _hgrn_consts()
    head = pl.BlockSpec((None, seq, HG_KEY), lambda b, h: (b, 0, h))
    return pl.pallas_call(
        _hgrn_kernel,
        grid=(bsz, HG_HEADS),
        in_specs=[_const_spec(m_f.shape), _const_spec(m_b.shape), _const_spec(lv_f.shape), _const_spec(lv_b.shape),
                  head, head, head, head, head, head, head,
                  pl.BlockSpec((1, HG_VAL), lambda b, h: (0, h))],
        out_specs=head,
        out_shape=jax.ShapeDtypeStruct((bsz, seq, HG_W), BF16),
        scratch_shapes=[pltpu.VMEM((seq, HG_VAL), F32)],
        compiler_params=pltpu.CompilerParams(dimension_semantics=("parallel", "parallel"),
                                             vmem_limit_bytes=VMEM_LIMIT),
        name="hgrn",
    )(jnp.asarray(m_f, BF16), jnp.asarray(m_b, BF16), jnp.asarray(lv_f), jnp.asarray(lv_b),
      qs, kkf, kkb, gf, gb, hv, gate, norm)


_FF_CHUNK = 1024


def _post_kernel(x_ref, at_ref, r_ref, p_ref, na_ref, wo_ref, n1_ref, n2_ref, wup_ref, wdn_ref, n3_ref,
                 wple_ref, wpg_ref, n4_ref, x_out):
    a = _rms(at_ref[...].astype(F32).T, na_ref[...]).astype(BF16)
    mix = _dot(a, wo_ref[:ATT_WIDTH, :]) + _dot(r_ref[...], wo_ref[ATT_WIDTH:, :])
    x = x_ref[...] + _rms(mix, n1_ref[...])
    h = _rms(x, n2_ref[...]).astype(BF16)
    m = jnp.zeros_like(x)
    for c0 in range(0, D_FF, _FF_CHUNK):
        act = jnp.square(jnp.maximum(_dot(h, wup_ref[:, c0:c0 + _FF_CHUNK]), 0.0)).astype(BF16)
        m = m + _dot(act, wdn_ref[c0:c0 + _FF_CHUNK, :])
    x = x + _rms(m, n3_ref[...])
    e = _dot(p_ref[...].astype(BF16), wple_ref[...])
    gate = jax.nn.sigmoid(_dot(x.astype(BF16), wpg_ref[...]))
    x_out[...] = x + _rms(e * gate, n4_ref[...])


def _post(x, at, r, p, na, wo, n1, n2, wup, wdn, n3, wple, wpg, n4, *, tm):
    n = x.shape[0]
    nt = at.shape[2] // tm
    row = lambda w: pl.BlockSpec((tm, w), lambda i: (i, 0))
    at_spec = pl.BlockSpec((None, ATT_WIDTH, tm), lambda i: (i // nt, 0, i % nt))
    vec = _const_spec((1, D_MODEL))
    single = lambda w: pl.BlockSpec(w.shape, lambda i: (0, 0), pipeline_mode=pl.Buffered(1))
    return pl.pallas_call(
        _post_kernel,
        grid=(n // tm,),
        in_specs=[row(D_MODEL), at_spec, row(HG_W), row(PLE_DIM), _const_spec((1, ATT_WIDTH)), single(wo), vec, vec,
                  single(wup), single(wdn), vec, single(wple), single(wpg), vec],
        out_specs=row(D_MODEL),
        out_shape=jax.ShapeDtypeStruct((n, D_MODEL), F32),
        compiler_params=pltpu.CompilerParams(dimension_semantics=("parallel",), vmem_limit_bytes=VMEM_LIMIT),
        name="post",
    )(x, at, r, p, na, wo, n1, n2, wup, wdn, n3, wple, wpg, n4)


def _layout_w_in(w):
    cq_ckv = w[:, :Q_LORA + KV_LORA]
    kr = w[:, Q_LORA + KV_LORA:Q_LORA + KV_LORA + ROPE_DIM]
    rest = w[:, Q_LORA + KV_LORA + ROPE_DIM:]
    half = ROPE_DIM // 2
    pad = jnp.zeros((w.shape[0], LANES - ROPE_DIM), w.dtype)
    kr_swap = jnp.concatenate([-kr[:, half:], kr[:, :half]], axis=1)
    return jnp.concatenate([cq_ckv, kr, pad, kr_swap, pad, rest], axis=1)


def _layout_w_uq(w):
    half = ROPE_DIM // 2
    w = w.reshape(Q_LORA, ATT_HEADS, QK_DIM)
    nope, x1, x2 = w[..., :NOPE_DIM], w[..., NOPE_DIM:NOPE_DIM + half], w[..., NOPE_DIM + half:]
    pad = jnp.zeros((Q_LORA, ATT_HEADS, HEAD_SLAB - QK_DIM), w.dtype)
    plain = jnp.concatenate([nope, x1, x2, pad], axis=-1)
    swap = jnp.concatenate([jnp.zeros_like(nope), -x2, x1, pad], axis=-1)
    return jnp.concatenate([plain.reshape(Q_LORA, QK_SLAB), swap.reshape(Q_LORA, QK_SLAB)], axis=1)


def _layout_w_ukv(w):
    w = w.reshape(KV_LORA, ATT_HEADS, NOPE_DIM + V_DIM)
    pad = jnp.zeros((KV_LORA, ATT_HEADS, HEAD_SLAB - NOPE_DIM), w.dtype)
    k = jnp.concatenate([w[..., :NOPE_DIM], pad], axis=-1).reshape(KV_LORA, QK_SLAB)
    vpad = jnp.zeros((KV_LORA, ATT_HEADS, V_ROWS - V_DIM), w.dtype)
    v = jnp.concatenate([w[..., NOPE_DIM:], vpad], axis=-1).reshape(KV_LORA, VT_ROWS)
    return jnp.concatenate([k, v], axis=1)


def _value_ones():
    o = np.zeros((ATT_HEADS, V_ROWS), np.float32)
    o[:, V_DIM:] = 1.0
    return o.reshape(1, VT_ROWS)


def _rope_placement():
    e = np.zeros((LANES, QK_SLAB), np.float32)
    for hd in range(ATT_HEADS):
        e[np.arange(ROPE_DIM), hd * HEAD_SLAB + NOPE_DIM + np.arange(ROPE_DIM)] = 1.0
    return e


def _rope_tables(seq):
    inv = 1.0 / (ROPE_THETA ** (jnp.arange(0, ROPE_DIM, 2, dtype=F32) / ROPE_DIM))
    ang = jnp.arange(seq, dtype=F32)[:, None] * inv[None, :]
    cos, sin = jnp.cos(ang), jnp.sin(ang)
    scale = jnp.asarray(QK_DIM ** -0.5 * LOG2E, F32)
    z = lambda w: jnp.zeros((seq, w), F32)
    cq = jnp.concatenate([jnp.ones((seq, NOPE_DIM), F32), cos, cos, z(HEAD_SLAB - QK_DIM)], axis=1) * scale
    sq = jnp.concatenate([z(NOPE_DIM), sin, sin, z(HEAD_SLAB - QK_DIM)], axis=1) * scale
    ck = jnp.concatenate([cos, cos, z(LANES - ROPE_DIM)], axis=1)
    sk = jnp.concatenate([sin, sin, z(LANES - ROPE_DIM)], axis=1)
    return cq, sq, ck, sk


def _trunk(x, p, lbs, norm_mix_pre, w_in, q_norm, w_uq, kv_norm, w_ukv, att_out_norm, hg_out_norm, w_o,
           norm_mix_post, norm_mlp_pre, w_up, w_down, norm_mlp_post, w_ple, w_ple_gate, norm_ple):
    bsz, seq, _ = x.shape
    n = bsz * seq
    tm = min(512, seq)
    tq = min(1024, seq)
    cq, sq, ck, sk = _rope_tables(seq)
    e1 = jnp.asarray(_rope_placement(), BF16)
    ones = jnp.asarray(_value_ones())
    row = lambda v: v.reshape(1, -1).astype(F32)
    xf = x.reshape(n, D_MODEL)
    for i in range(DEPTH):
        q, k, vt, hq, kkf, kkb, gf, gb, hv, gate = _in_proj(
            xf, row(norm_mix_pre[i]), _layout_w_in(w_in[i]).astype(BF16), row(q_norm[i]), row(kv_norm[i]),
            _layout_w_uq(w_uq[i]).astype(BF16), _layout_w_ukv(w_ukv[i]).astype(BF16), e1, ones,
            cq, sq, ck, sk, row(lbs[0, i]), row(lbs[1, i]), seq=seq, tm=tm)
        b3 = lambda t: t.reshape(bsz, seq, t.shape[-1])
        at = _attention(b3(q), b3(k), vt, tq=tq)
        r = _hgrn(b3(hq), b3(kkf), b3(kkb), b3(gf), b3(gb), b3(hv), b3(gate), row(hg_out_norm[i]))
        xf = _post(xf, at, r.reshape(n, HG_W), p[i].reshape(n, PLE_DIM), row(att_out_norm[i]),
                   w_o[i].astype(BF16), row(norm_mix_post[i]), row(norm_mlp_pre[i]), w_up[i].astype(BF16),
                   w_down[i].astype(BF16), row(norm_mlp_post[i]), w_ple[i].astype(BF16),
                   w_ple_gate[i].astype(BF16), row(norm_ple[i]), tm=min(2 * tm, seq))
    return xf.reshape(bsz, seq, D_MODEL)


def kernel(x_prompt, x_sample, p_prompt, p_sample, norm_mix_pre, w_in, q_norm, w_uq, kv_norm, w_ukv, att_out_norm, hg_lb, hg_out_norm, w_o, norm_mix_post, norm_mlp_pre, w_up, w_down, norm_mlp_post, w_ple, w_ple_gate, norm_ple):
    lbs = jnp.cumsum(jax.nn.softmax(hg_lb.astype(F32), axis=1), axis=1)
    lbs = lbs - lbs[:, :1]
    ws = (norm_mix_pre, w_in, q_norm, w_uq, kv_norm, w_ukv, att_out_norm, hg_out_norm, w_o, norm_mix_post,
          norm_mlp_pre, w_up, w_down, norm_mlp_post, w_ple, w_ple_gate, norm_ple)
    return (_trunk(x_prompt, p_prompt, lbs, *ws), _trunk(x_sample, p_sample, lbs, *ws))
```

```python
import functools

import numpy as np
import jax
import jax.numpy as jnp
from jax import lax
from jax.experimental import pallas as pl
from jax.experimental.pallas import tpu as pltpu

D_MODEL = 1024
DEPTH = 4
ATT_HEADS = 8
Q_LORA = 256
KV_LORA = 128
NOPE_DIM = 64
ROPE_DIM = 32
V_DIM = 64
QK_DIM = NOPE_DIM + ROPE_DIM
ATT_WIDTH = ATT_HEADS * V_DIM
HG_HEADS = 4
HG_KEY = 128
HG_VAL = 128
HG_W = HG_HEADS * HG_KEY
D_FF = 4 * D_MODEL
PLE_DIM = 256
ROPE_THETA = 10000.0
EPS = 1e-6

LANES = 128
HEAD_SLAB = LANES
QK_SLAB = ATT_HEADS * HEAD_SLAB
V_ROWS = V_DIM + 16
VT_ROWS = ATT_HEADS * V_ROWS
LOG2E = 1.4426950408889634
HG_CHUNK = 128
ATT_HEADROOM = 96.0
VMEM_LIMIT = 56 * 1024 * 1024

F32 = jnp.float32
BF16 = jnp.bfloat16

_C_CQ = 0
_C_CKV = _C_CQ + Q_LORA
_C_KR = _C_CKV + KV_LORA
_C_KRS = _C_KR + LANES
_C_HQ = _C_KRS + LANES
_C_HFF = _C_HQ + HG_W
_C_HFB = _C_HFF + HG_W
_C_HI = _C_HFB + HG_W
_C_HG = _C_HI + HG_W
_IN_COLS_P = _C_HG + HG_W


def _rms(x, gain):
    return x * lax.rsqrt(jnp.mean(x * x, axis=-1, keepdims=True) + EPS) * gain


def _dot(a, b):
    return jnp.dot(a, b, preferred_element_type=F32)


def _dot_nt(a, b):
    return lax.dot_general(a, b, (((1,), (1,)), ((), ())), preferred_element_type=F32)


def _dot_tn(a, b):
    return lax.dot_general(a, b, (((0,), (0,)), ((), ())), preferred_element_type=F32)


def _const_spec(shape):
    return pl.BlockSpec(shape, lambda *_: (0,) * len(shape))


def _in_proj_kernel(x_ref, g_ref, win_ref, qn_ref, kvn_ref, wuq_ref, wukv_ref, e1_ref, ones_ref,
                    cq_ref, sq_ref, ck_ref, sk_ref, lbf_ref, lbb_ref,
                    q_out, k_out, v_out, hq_out, kkf_out, kkb_out, gf_out, gb_out, hv_out, gate_out):
    h = _rms(x_ref[...], g_ref[...]).astype(BF16)
    z = _dot(h, win_ref[...])

    qn = _rms(z[:, _C_CQ:_C_CKV], qn_ref[...]).astype(BF16)
    q2 = _dot(qn, wuq_ref[...])
    cq = cq_ref[...]
    sq = sq_ref[...]
    for hd in range(ATT_HEADS):
        a = q2[:, hd * HEAD_SLAB:(hd + 1) * HEAD_SLAB]
        b = q2[:, QK_SLAB + hd * HEAD_SLAB:QK_SLAB + (hd + 1) * HEAD_SLAB]
        q_out[:, hd * HEAD_SLAB:(hd + 1) * HEAD_SLAB] = (a * cq + b * sq).astype(BF16)

    kvn = _rms(z[:, _C_CKV:_C_KR], kvn_ref[...]).astype(BF16)
    kv2 = _dot(kvn, wukv_ref[...])
    kr = z[:, _C_KR:_C_KRS] * ck_ref[...] + z[:, _C_KRS:_C_HQ] * sk_ref[...]
    krp = _dot(kr.astype(BF16), e1_ref[...])
    k_out[...] = (kv2[:, :QK_SLAB] + krp).astype(BF16)
    v_out[...] = (kv2[:, QK_SLAB:] + ones_ref[...]).T.astype(BF16)

    hq = z[:, _C_HQ:_C_HFF]
    hq_out[...] = (hq * jax.nn.sigmoid(hq)).astype(BF16)
    for src, lb_ref, kk_o, g_o in ((_C_HFF, lbf_ref, kkf_out, gf_out), (_C_HFB, lbb_ref, kkb_out, gb_out)):
        lb = lb_ref[...]
        f = lb + (1.0 - lb) * jax.nn.sigmoid(z[:, src:src + HG_W])
        g_o[...] = jnp.log2(f)
        kk_o[...] = (1.0 - f).astype(BF16)
    hv_out[...] = z[:, _C_HI:_C_HG].astype(BF16)
    hg = z[:, _C_HG:_IN_COLS_P]
    gate_out[...] = (hg * jax.nn.sigmoid(hg)).astype(BF16)


def _in_proj(x, gain, win, qn, kvn, wuq, wukv, e1, ones, cq, sq, ck, sk, lbf, lbb, *, seq, tm):
    n = x.shape[0]
    nt = seq // tm
    row = lambda w: pl.BlockSpec((tm, w), lambda i: (i, 0))
    tab = pl.BlockSpec((tm, LANES), lambda i: (i % nt, 0))
    outs = [(QK_SLAB, BF16), (QK_SLAB, BF16), None, (HG_W, BF16), (HG_W, BF16), (HG_W, BF16),
            (HG_W, F32), (HG_W, F32), (HG_W, BF16), (HG_W, BF16)]
    vt_spec = pl.BlockSpec((None, VT_ROWS, tm), lambda i: (i, 0, 0))
    vt_shape = jax.ShapeDtypeStruct((n // tm, VT_ROWS, tm), BF16)
    return pl.pallas_call(
        _in_proj_kernel,
        grid=(n // tm,),
        in_specs=[row(D_MODEL), _const_spec((1, D_MODEL)), _const_spec(win.shape),
                  _const_spec((1, Q_LORA)), _const_spec((1, KV_LORA)), _const_spec(wuq.shape),
                  _const_spec(wukv.shape), _const_spec(e1.shape), _const_spec(ones.shape), tab, tab, tab, tab,
                  _const_spec((1, HG_W)), _const_spec((1, HG_W))],
        out_specs=[vt_spec if o is None else row(o[0]) for o in outs],
        out_shape=[vt_shape if o is None else jax.ShapeDtypeStruct((n, o[0]), o[1]) for o in outs],
        compiler_params=pltpu.CompilerParams(dimension_semantics=("parallel",), vmem_limit_bytes=VMEM_LIMIT),
        name="in_proj",
    )(x, gain, win, qn, kvn, wuq, wukv, e1, ones, cq, sq, ck, sk, lbf, lbb)


def _attention_kernel(q_ref, k_ref, vt_ref, ot_out, p_scr, st_scr, acc_scr):
    nkb, _, kb = vt_ref.shape
    tq = p_scr.shape[2]
    steps = [(i, j) for i in range(q_ref.shape[0] // tq) for j in range(nkb)]

    def qk(n):
        i, j = steps[n]
        return _dot_nt(k_ref[j * kb:(j + 1) * kb, :], q_ref[i * tq:(i + 1) * tq, :])

    def finish_tile(i):
        acc = acc_scr[...]
        ot_out[:, i * tq:(i + 1) * tq] = (acc[:V_DIM] / acc[V_DIM:V_DIM + 1]).astype(BF16)
        return acc

    def stream(n, ref):
        st = qk(n)
        p_scr[n % 2] = jnp.exp2(st - ref).astype(BF16)
        return jnp.max(st, axis=0, keepdims=True)

    zero = jnp.zeros((1, tq), F32)
    ref = zero
    bmax = stream(0, ref)
    used = jnp.abs(bmax)
    for n, (i, j) in enumerate(steps):
        last = n + 1 == len(steps)
        if not last:
            first_of_tile = steps[n + 1][1] == 0
            next_ref = zero if first_of_tile else jnp.maximum(ref, bmax)
            next_bmax = stream(n + 1, next_ref)
            used = jnp.maximum(used, jnp.abs(next_bmax) if first_of_tile else next_bmax - next_ref)
        pv = _dot(vt_ref[j], p_scr[n % 2])
        acc_scr[...] = pv if j == 0 else acc_scr[...] + pv
        if j == nkb - 1:
            acc = finish_tile(i)
            not_finite = jnp.where(jnp.abs(acc) < jnp.inf, 0.0, jnp.inf)
            used = jnp.maximum(used, jnp.max(not_finite, axis=0, keepdims=True))
        elif not last:
            acc_scr[...] = acc_scr[...] * jnp.exp2(ref - next_ref)
        if not last:
            ref, bmax = next_ref, next_bmax

    @pl.when(jnp.logical_not(jnp.max(used) <= ATT_HEADROOM))
    def _():
        slots = st_scr.shape[0]
        for n in range(slots - 1):
            st_scr[n] = qk(n)
        m = None
        for n, (i, j) in enumerate(steps):
            if n + slots - 1 < len(steps):
                st_scr[(n + slots - 1) % slots] = qk(n + slots - 1)
            st = st_scr[n % slots]
            blk = jnp.max(st, axis=0, keepdims=True)
            m_new = blk if j == 0 else jnp.maximum(m, blk)
            pv = _dot(vt_ref[j], jnp.exp2(st - m_new).astype(BF16))
            acc_scr[...] = pv if j == 0 else jnp.exp2(m - m_new) * acc_scr[...] + pv
            m = m_new
            if j == nkb - 1:
                finish_tile(i)


def _attention(q, k, vt, *, tq):
    bsz, seq, _ = q.shape
    nkb = vt.shape[0] // bsz
    kb = vt.shape[2]
    head = pl.BlockSpec((None, seq, HEAD_SLAB), lambda b, h: (b, 0, h))
    return pl.pallas_call(
        _attention_kernel,
        grid=(bsz, ATT_HEADS),
        in_specs=[head, head, pl.BlockSpec((nkb, V_ROWS, kb), lambda b, h: (b, h, 0))],
        out_specs=pl.BlockSpec((None, V_DIM, seq), lambda b, h: (b, h, 0)),
        out_shape=jax.ShapeDtypeStruct((bsz, ATT_WIDTH, seq), BF16),
        scratch_shapes=[pltpu.VMEM((2, kb, tq), BF16), pltpu.VMEM((3, kb, tq), F32), pltpu.VMEM((V_ROWS, tq), F32)],
        compiler_params=pltpu.CompilerParams(dimension_semantics=("parallel", "parallel"),
                                             vmem_limit_bytes=VMEM_LIMIT),
        name="attention",
    )(q, k, vt)


_HG_LEVELS = int(np.log2(HG_CHUNK))
_HG_DIAG = _HG_LEVELS
_HG_SUBLANES = 8


@functools.lru_cache(maxsize=None)
def _hgrn_consts():
    c = HG_CHUNK
    t = np.arange(c)[:, None]
    u = np.arange(c)[None, :]
    blocks = [u <= t]
    level = np.full((c, c), -1, np.int32)
    level[np.arange(c), np.arange(c)] = _HG_DIAG
    for li in range(_HG_LEVELS):
        h = (c // 2) >> li
        mid = (t // (2 * h)) * (2 * h) + h - 1
        upper = (t % (2 * h)) >= h
        if 1 < h < _HG_SUBLANES:
            blocks.append((upper & (u > mid) & (u <= t)) | (~upper & (u > t) & (u <= mid)))
        same = (t // (2 * h)) == (u // (2 * h))
        level[same & upper & ((u % (2 * h)) < h)] = li
    m_f = np.concatenate(blocks, 0).astype(np.float32)
    m_b = np.concatenate([blk[::-1, ::-1] for blk in blocks], 0).astype(np.float32)
    return m_f, m_b, level, np.ascontiguousarray(level.T)


def _hgrn_chunk(rows, backward, m_ref, lv_ref, q_ref, kk_ref, g_ref, v_ref, st):
    c = HG_CHUNK
    g = g_ref[rows, :]
    g_hi = g.astype(BF16)
    g_lo = (g - g_hi.astype(F32)).astype(BF16)
    r = _dot(m_ref[...], jnp.concatenate([g_hi, g_lo], axis=1))
    x = r[:, :HG_KEY] + r[:, HG_KEY:]
    b = x[0:c]
    q = q_ref[rows, :].astype(F32)
    kk = kk_ref[rows, :].astype(F32)
    v = v_ref[rows, :]
    lv = lv_ref[...]
    tbit = lax.broadcasted_iota(jnp.int32, (c, HG_KEY), 0)

    tile = _HG_SUBLANES
    lvr = [lv[r * tile:(r + 1) * tile] for r in range(c // tile)]
    diag = jnp.sum(q * kk, axis=-1, keepdims=True)
    srows = [jnp.where(lvr[r] == _HG_DIAG, diag[r * tile:(r + 1) * tile], 0.0) for r in range(c // tile)]
    kk_prev = pltpu.roll(kk, (c - 1) if backward else 1, axis=0)
    near = jnp.sum(q * kk_prev * jnp.exp2(g), axis=-1, keepdims=True)
    small = 1
    for li in range(_HG_LEVELS):
        h = (c // 2) >> li
        if h == 1:
            for r in range(c // tile):
                srows[r] = jnp.where(lvr[r] == li, near[r * tile:(r + 1) * tile], srows[r])
            continue
        if h >= tile:
            parts, qparts, qrows = [], [], []
            for s0 in range(0, c, 2 * h):
                lo, hi = slice(s0, s0 + h), slice(s0 + h, s0 + 2 * h)
                if backward:
                    bm = b[s0 + h:s0 + h + 1]
                    pq, pk = q[lo] * jnp.exp2(b[lo] - bm), kk[hi] * jnp.exp2(bm - b[hi])
                    parts += [pq, pk]
                    qrows += range(s0 // tile, (s0 + h) // tile)
                else:
                    bm = b[s0 + h - 1:s0 + h]
                    pk, pq = kk[lo] * jnp.exp2(bm - b[lo]), q[hi] * jnp.exp2(b[hi] - bm)
                    parts += [pk, pq]
                    qrows += range((s0 + h) // tile, (s0 + 2 * h) // tile)
                qparts.append(pq)
            gram = _dot_nt(jnp.concatenate(qparts, axis=0).astype(BF16),
                           jnp.concatenate(parts, axis=0).astype(BF16))
        else:
            is_query = ((tbit & h) == 0) if backward else ((tbit & h) != 0)
            p = (jnp.where(is_query, q, kk) * jnp.exp2(x[small * c:(small + 1) * c])).astype(BF16)
            small += 1
            gram = _dot_nt(p, p)
            qrows = range(c // tile)
        for i, r in enumerate(qrows):
            srows[r] = jnp.where(lvr[r] == li, gram[i * tile:(i + 1) * tile], srows[r])
    scores = jnp.concatenate(srows, axis=0)

    b_end = b[0:1] if backward else b[c - 1:c]
    o = _dot(scores.astype(BF16), v)
    o = o + _dot_nt((q * jnp.exp2(b)).astype(BF16), st.astype(BF16))
    kd = (kk * jnp.exp2(b_end - b)).astype(BF16)
    st_new = st * jnp.exp2(b_end) + _dot_tn(v, kd)
    return o, st_new


def _hgrn_kernel(mf_ref, mb_ref, lvf_ref, lvb_ref, q_ref, kkf_ref, kkb_ref, gf_ref, gb_ref, v_ref,
                 gate_ref, norm_ref, r_out, acc):
    c = HG_CHUNK
    nc = q_ref.shape[0] // c
    half = nc // 2

    def both(i, st_f, st_b):
        rows_f = pl.ds(pl.multiple_of(i * c, c), c)
        rows_b = pl.ds(pl.multiple_of((nc - 1 - i) * c, c), c)
        o_f, st_f = _hgrn_chunk(rows_f, False, mf_ref, lvf_ref, q_ref, kkf_ref, gf_ref, v_ref, st_f)
        o_b, st_b = _hgrn_chunk(rows_b, True, mb_ref, lvb_ref, q_ref, kkb_ref, gb_ref, v_ref, st_b)
        return rows_f, rows_b, o_f, o_b, st_f, st_b

    def first_touch(i, carry):
        rows_f, rows_b, o_f, o_b, st_f, st_b = both(i, *carry)
        acc[rows_f, :] = o_f
        acc[rows_b, :] = o_b
        return st_f, st_b

    def finish(rows, o):
        tot = acc[rows, :] + o
        r_out[rows, :] = (_rms(tot, norm_ref[...]) * gate_ref[rows, :].astype(F32)).astype(BF16)

    def second_touch(i, carry):
        rows_f, rows_b, o_f, o_b, st_f, st_b = both(i, *carry)
        finish(rows_f, o_f)
        finish(rows_b, o_b)
        return st_f, st_b

    zero = jnp.zeros((HG_VAL, HG_KEY), F32)
    carry = lax.fori_loop(0, half, first_touch, (zero, zero), unroll=4)
    lax.fori_loop(half, nc, second_touch, carry, unroll=4)


def _hgrn(qs, kkf, kkb, gf, gb, hv, gate, norm):
    bsz, seq, _ = qs.shape
    assert seq % (8 * HG_CHUNK) == 0
    m_f, m_b, lv_f, lv_b = _hgrn_consts()
    head = pl.BlockSpec((None, seq, HG_KEY), lambda b, h: (b, 0, h))
    return pl.pallas_call(
        _hgrn_kernel,
        grid=(bsz, HG_HEADS),
        in_specs=[_const_spec(m_f.shape), _const_spec(m_b.shape), _const_spec(lv_f.shape), _const_spec(lv_b.shape),
                  head, head, head, head, head, head, head,
                  pl.BlockSpec((1, HG_VAL), lambda b, h: (0, h))],
        out_specs=head,
        out_shape=jax.ShapeDtypeStruct((bsz, seq, HG_W), BF16),
        scratch_shapes=[pltpu.VMEM((seq, HG_VAL), F32)],
        compiler_params=pltpu.CompilerParams(dimension_semantics=("parallel", "parallel"),
                                             vmem_limit_bytes=VMEM_LIMIT),
        name="hgrn",
    )(jnp.asarray(m_f, BF16), jnp.asarray(m_b, BF16), jnp.asarray(lv_f), jnp.asarray(lv_b),
      qs, kkf, kkb, gf, gb, hv, gate, norm)


_FF_CHUNK = 1024


def _post_kernel(x_ref, at_ref, r_ref, p_ref, na_ref, wo_ref, n1_ref, n2_ref, wup_ref, wdn_ref, n3_ref,
                 wple_ref, wpg_ref, n4_ref, x_out):
    a = _rms(at_ref[...].astype(F32).T, na_ref[...]).astype(BF16)
    mix = _dot(a, wo_ref[:ATT_WIDTH, :]) + _dot(r_ref[...], wo_ref[ATT_WIDTH:, :])
    x = x_ref[...] + _rms(mix, n1_ref[...])
    h = _rms(x, n2_ref[...]).astype(BF16)
    m = jnp.zeros_like(x)
    for c0 in range(0, D_FF, _FF_CHUNK):
        act = jnp.square(jnp.maximum(_dot(h, wup_ref[:, c0:c0 + _FF_CHUNK]), 0.0)).astype(BF16)
        m = m + _dot(act, wdn_ref[c0:c0 + _FF_CHUNK, :])
    x = x + _rms(m, n3_ref[...])
    e = _dot(p_ref[...].astype(BF16), wple_ref[...])
    gate = jax.nn.sigmoid(_dot(x.astype(BF16), wpg_ref[...]))
    x_out[...] = x + _rms(e * gate, n4_ref[...])


def _post(x, at, r, p, na, wo, n1, n2, wup, wdn, n3, wple, wpg, n4, *, tm):
    n = x.shape[0]
    nt = at.shape[2] // tm
    row = lambda w: pl.BlockSpec((tm, w), lambda i: (i, 0))
    at_spec = pl.BlockSpec((None, ATT_WIDTH, tm), lambda i: (i // nt, 0, i % nt))
    vec = _const_spec((1, D_MODEL))
    single = lambda w: pl.BlockSpec(w.shape, lambda i: (0, 0), pipeline_mode=pl.Buffered(1))
    return pl.pallas_call(
        _post_kernel,
        grid=(n // tm,),
        in_specs=[row(D_MODEL), at_spec, row(HG_W), row(PLE_DIM), _const_spec((1, ATT_WIDTH)), single(wo), vec, vec,
                  single(wup), single(wdn), vec, single(wple), single(wpg), vec],
        out_specs=row(D_MODEL),
        out_shape=jax.ShapeDtypeStruct((n, D_MODEL), F32),
        compiler_params=pltpu.CompilerParams(dimension_semantics=("parallel",), vmem_limit_bytes=VMEM_LIMIT),
        name="post",
    )(x, at, r, p, na, wo, n1, n2, wup, wdn, n3, wple, wpg, n4)


def _layout_w_in(w):
    cq_ckv = w[:, :Q_LORA + KV_LORA]
    kr = w[:, Q_LORA + KV_LORA:Q_LORA + KV_LORA + ROPE_DIM]
    rest = w[:, Q_LORA + KV_LORA + ROPE_DIM:]
    half = ROPE_DIM // 2
    pad = jnp.zeros((w.shape[0], LANES - ROPE_DIM), w.dtype)
    kr_swap = jnp.concatenate([-kr[:, half:], kr[:, :half]], axis=1)
    return jnp.concatenate([cq_ckv, kr, pad, kr_swap, pad, rest], axis=1)


def _layout_w_uq(w):
    half = ROPE_DIM // 2
    w = w.reshape(Q_LORA, ATT_HEADS, QK_DIM)
    nope, x1, x2 = w[..., :NOPE_DIM], w[..., NOPE_DIM:NOPE_DIM + half], w[..., NOPE_DIM + half:]
    pad = jnp.zeros((Q_LORA, ATT_HEADS, HEAD_SLAB - QK_DIM), w.dtype)
    plain = jnp.concatenate([nope, x1, x2, pad], axis=-1)
    swap = jnp.concatenate([jnp.zeros_like(nope), -x2, x1, pad], axis=-1)
    return jnp.concatenate([plain.reshape(Q_LORA, QK_SLAB), swap.reshape(Q_LORA, QK_SLAB)], axis=1)


def _layout_w_ukv(w):
    w = w.reshape(KV_LORA, ATT_HEADS, NOPE_DIM + V_DIM)
    pad = jnp.zeros((KV_LORA, ATT_HEADS, HEAD_SLAB - NOPE_DIM), w.dtype)
    k = jnp.concatenate([w[..., :NOPE_DIM], pad], axis=-1).reshape(KV_LORA, QK_SLAB)
    vpad = jnp.zeros((KV_LORA, ATT_HEADS, V_ROWS - V_DIM), w.dtype)
    v = jnp.concatenate([w[..., NOPE_DIM:], vpad], axis=-1).reshape(KV_LORA, VT_ROWS)
    return jnp.concatenate([k, v], axis=1)


def _value_ones():
    o = np.zeros((ATT_HEADS, V_ROWS), np.float32)
    o[:, V_DIM:] = 1.0
    return o.reshape(1, VT_ROWS)


def _rope_placement():
    e = np.zeros((LANES, QK_SLAB), np.float32)
    for hd in range(ATT_HEADS):
        e[np.arange(ROPE_DIM), hd * HEAD_SLAB + NOPE_DIM + np.arange(ROPE_DIM)] = 1.0
    return e


def _rope_tables(seq):
    inv = 1.0 / (ROPE_THETA ** (jnp.arange(0, ROPE_DIM, 2, dtype=F32) / ROPE_DIM))
    ang = jnp.arange(seq, dtype=F32)[:, None] * inv[None, :]
    cos, sin = jnp.cos(ang), jnp.sin(ang)
    scale = jnp.asarray(QK_DIM ** -0.5 * LOG2E, F32)
    z = lambda w: jnp.zeros((seq, w), F32)
    cq = jnp.concatenate([jnp.ones((seq, NOPE_DIM), F32), cos, cos, z(HEAD_SLAB - QK_DIM)], axis=1) * scale
    sq = jnp.concatenate([z(NOPE_DIM), sin, sin, z(HEAD_SLAB - QK_DIM)], axis=1) * scale
    ck = jnp.concatenate([cos, cos, z(LANES - ROPE_DIM)], axis=1)
    sk = jnp.concatenate([sin, sin, z(LANES - ROPE_DIM)], axis=1)
    return cq, sq, ck, sk


def _trunk(x, p, lbs, norm_mix_pre, w_in, q_norm, w_uq, kv_norm, w_ukv, att_out_norm, hg_out_norm, w_o,
           norm_mix_post, norm_mlp_pre, w_up, w_down, norm_mlp_post, w_ple, w_ple_gate, norm_ple):
    bsz, seq, _ = x.shape
    n = bsz * seq
    tm = min(512, seq)
    tq = min(1024, seq)
    cq, sq, ck, sk = _rope_tables(seq)
    e1 = jnp.asarray(_rope_placement(), BF16)
    ones = jnp.asarray(_value_ones())
    row = lambda v: v.reshape(1, -1).astype(F32)
    xf = x.reshape(n, D_MODEL)
    for i in range(DEPTH):
        q, k, vt, hq, kkf, kkb, gf, gb, hv, gate = _in_proj(
            xf, row(norm_mix_pre[i]), _layout_w_in(w_in[i]).astype(BF16), row(q_norm[i]), row(kv_norm[i]),
            _layout_w_uq(w_uq[i]).astype(BF16), _layout_w_ukv(w_ukv[i]).astype(BF16), e1, ones,
            cq, sq, ck, sk, row(lbs[0, i]), row(lbs[1, i]), seq=seq, tm=tm)
        b3 = lambda t: t.reshape(bsz, seq, t.shape[-1])
        at = _attention(b3(q), b3(k), vt, tq=tq)
        r = _hgrn(b3(hq), b3(kkf), b3(kkb), b3(gf), b3(gb), b3(hv), b3(gate), row(hg_out_norm[i]))
        xf = _post(xf, at, r.reshape(n, HG_W), p[i].reshape(n, PLE_DIM), row(att_out_norm[i]),
                   w_o[i].astype(BF16), row(norm_mix_post[i]), row(norm_mlp_pre[i]), w_up[i].astype(BF16),
                   w_down[i].astype(BF16), row(norm_mlp_post[i]), w_ple[i].astype(BF16),
                   w_ple_gate[i].astype(BF16), row(norm_ple[i]), tm=min(2 * tm, seq))
    return xf.reshape(bsz, seq, D_MODEL)


def kernel(x_prompt, x_sample, p_prompt, p_sample, norm_mix_pre, w_in, q_norm, w_uq, kv_norm, w_ukv, att_out_norm, hg_lb, hg_out_norm, w_o, norm_mix_post, norm_mlp_pre, w_up, w_down, norm_mlp_post, w_ple, w_ple_gate, norm_ple):
    lbs = jnp.cumsum(jax.nn.softmax(hg_lb.astype(F32), axis=1), axis=1)
    lbs = lbs - lbs[:, :1]
    ws = (norm_mix_pre, w_in, q_norm, w_uq, kv_norm, w_ukv, att_out_norm, hg_out_norm, w_o, norm_mix_post,
          norm_mlp_pre, w_up, w_down, norm_mlp_post, w_ple, w_ple_gate, norm_ple)
    return (_trunk(x_prompt, p_prompt, lbs, *ws), _trunk(x_sample, p_sample, lbs, *ws))
```

```python
import functools

import numpy as np
import jax
import jax.numpy as jnp
from jax import lax
from jax.experimental import pallas as pl
from jax.experimental.pallas import tpu as pltpu

D_MODEL = 1024
DEPTH = 4
ATT_HEADS = 8
Q_LORA = 256
KV_LORA = 128
NOPE_DIM = 64
ROPE_DIM = 32
V_DIM = 64
QK_DIM = NOPE_DIM + ROPE_DIM
ATT_WIDTH = ATT_HEADS * V_DIM
HG_HEADS = 4
HG_KEY = 128
HG_VAL = 128
HG_W = HG_HEADS * HG_KEY
D_FF = 4 * D_MODEL
PLE_DIM = 256
ROPE_THETA = 10000.0
EPS = 1e-6

LANES = 128
HEAD_SLAB = LANES
QK_SLAB = ATT_HEADS * HEAD_SLAB
BF16_SUBLANES = 16
V_ROWS = V_DIM + BF16_SUBLANES
VT_ROWS = ATT_HEADS * V_ROWS
LOG2E = 1.4426950408889634
HG_CHUNK = 128
ATT_HEADROOM = 96.0
VMEM_LIMIT = 56 * 1024 * 1024

F32 = jnp.float32
BF16 = jnp.bfloat16

_C_CQ = 0
_C_CKV = _C_CQ + Q_LORA
_C_KR = _C_CKV + KV_LORA
_C_HQ = _C_KR + LANES
_C_HFF = _C_HQ + HG_W
_C_HFB = _C_HFF + HG_W
_C_HI = _C_HFB + HG_W
_C_HG = _C_HI + HG_W
_IN_COLS_P = _C_HG + HG_W


def _rms(x, gain):
    return x * lax.rsqrt(jnp.mean(x * x, axis=-1, keepdims=True) + EPS) * gain


def _dot(a, b):
    return jnp.dot(a, b, preferred_element_type=F32)


def _dot_nt(a, b):
    return lax.dot_general(a, b, (((1,), (1,)), ((), ())), preferred_element_type=F32)


def _dot_tn(a, b):
    return lax.dot_general(a, b, (((0,), (0,)), ((), ())), preferred_element_type=F32)


def _const_spec(shape):
    return pl.BlockSpec(shape, lambda *_: (0,) * len(shape))


def _rope(x, cos, sin_lo, sin_hi):
    half = ROPE_DIM // 2
    return x * cos + pltpu.roll(x, LANES - half, axis=1) * sin_lo + pltpu.roll(x, half, axis=1) * sin_hi


def _in_proj_kernel(x_ref, g_ref, win_ref, qn_ref, kvn_ref, wuq_ref, wukv_ref, ones_ref,
                    cq_ref, sqa_ref, sqb_ref, ck_ref, ska_ref, skb_ref, lbf_ref, lbb_ref,
                    q_out, k_out, v_out, hq_out, kkf_out, kkb_out, gf_out, gb_out, hv_out, gate_out):
    h = _rms(x_ref[...], g_ref[...]).astype(BF16)
    z = _dot(h, win_ref[...])

    qn = _rms(z[:, _C_CQ:_C_CKV], qn_ref[...]).astype(BF16)
    q1 = _dot(qn, wuq_ref[...])
    cq, sqa, sqb = cq_ref[...], sqa_ref[...], sqb_ref[...]
    for hd in range(ATT_HEADS):
        slab = slice(hd * HEAD_SLAB, (hd + 1) * HEAD_SLAB)
        q_out[:, slab] = _rope(q1[:, slab], cq, sqa, sqb).astype(BF16)

    kvn = _rms(z[:, _C_CKV:_C_KR], kvn_ref[...]).astype(BF16)
    kv2 = _dot(kvn, wukv_ref[...])
    kr = _rope(z[:, _C_KR:_C_HQ], ck_ref[...], ska_ref[...], skb_ref[...])
    kr = pltpu.roll(kr, NOPE_DIM, axis=1)
    for hd in range(ATT_HEADS):
        slab = slice(hd * HEAD_SLAB, (hd + 1) * HEAD_SLAB)
        k_out[:, slab] = (kv2[:, slab] + kr).astype(BF16)
    v_out[...] = (kv2[:, QK_SLAB:] + ones_ref[...]).T.astype(BF16)

    hq = z[:, _C_HQ:_C_HFF]
    hq_out[...] = (hq * jax.nn.sigmoid(hq)).astype(BF16)
    for src, lb_ref, kk_o, g_o in ((_C_HFF, lbf_ref, kkf_out, gf_out), (_C_HFB, lbb_ref, kkb_out, gb_out)):
        lb = lb_ref[...]
        f = lb + (1.0 - lb) * jax.nn.sigmoid(z[:, src:src + HG_W])
        g_o[...] = jnp.log2(f)
        kk_o[...] = (1.0 - f).astype(BF16)
    hv_out[...] = z[:, _C_HI:_C_HG].astype(BF16)
    hg = z[:, _C_HG:_IN_COLS_P]
    gate_out[...] = (hg * jax.nn.sigmoid(hg)).astype(BF16)


def _in_proj(x, gain, win, qn, kvn, wuq, wukv, ones, tables, lbf, lbb, *, seq, tm):
    n = x.shape[0]
    nt = seq // tm
    row = lambda w: pl.BlockSpec((tm, w), lambda i: (i, 0))
    tab = pl.BlockSpec((tm, LANES), lambda i: (i % nt, 0))
    outs = [(QK_SLAB, BF16), (QK_SLAB, BF16), None, (HG_W, BF16), (HG_W, BF16), (HG_W, BF16),
            (HG_W, F32), (HG_W, F32), (HG_W, BF16), (HG_W, BF16)]
    vt_spec = pl.BlockSpec((None, VT_ROWS, tm), lambda i: (i, 0, 0))
    vt_shape = jax.ShapeDtypeStruct((n // tm, VT_ROWS, tm), BF16)
    return pl.pallas_call(
        _in_proj_kernel,
        grid=(n // tm,),
        in_specs=[row(D_MODEL), _const_spec((1, D_MODEL)), _const_spec(win.shape),
                  _const_spec((1, Q_LORA)), _const_spec((1, KV_LORA)), _const_spec(wuq.shape),
                  _const_spec(wukv.shape), _const_spec(ones.shape)] + [tab] * len(tables) + [
                  _const_spec((1, HG_W)), _const_spec((1, HG_W))],
        out_specs=[vt_spec if o is None else row(o[0]) for o in outs],
        out_shape=[vt_shape if o is None else jax.ShapeDtypeStruct((n, o[0]), o[1]) for o in outs],
        compiler_params=pltpu.CompilerParams(dimension_semantics=("parallel",), vmem_limit_bytes=VMEM_LIMIT),
        name="in_proj",
    )(x, gain, win, qn, kvn, wuq, wukv, ones, *tables, lbf, lbb)


def _attention_kernel(q_ref, k_ref, vt_ref, ot_out, p_scr, st_scr, acc_scr):
    nkb, _, kb = vt_ref.shape
    tq = p_scr.shape[2]
    steps = [(i, j) for i in range(q_ref.shape[0] // tq) for j in range(nkb)]

    def qk(n):
        i, j = steps[n]
        return _dot_nt(k_ref[j * kb:(j + 1) * kb, :], q_ref[i * tq:(i + 1) * tq, :])

    def finish_tile(i):
        acc = acc_scr[...]
        ot_out[:, i * tq:(i + 1) * tq] = (acc[:V_DIM] / acc[V_DIM:V_DIM + 1]).astype(BF16)
        return acc

    def stream(n, ref):
        st = qk(n)
        p_scr[n % 2] = jnp.exp2(st - ref).astype(BF16)
        return jnp.max(st, axis=0, keepdims=True)

    zero = jnp.zeros((1, tq), F32)
    ref = zero
    bmax = stream(0, ref)
    used = jnp.abs(bmax)
    for n, (i, j) in enumerate(steps):
        last = n + 1 == len(steps)
        if not last:
            first_of_tile = steps[n + 1][1] == 0
            next_ref = zero if first_of_tile else jnp.maximum(ref, bmax)
            next_bmax = stream(n + 1, next_ref)
            used = jnp.maximum(used, jnp.abs(next_bmax) if first_of_tile else next_bmax - next_ref)
        pv = _dot(vt_ref[j], p_scr[n % 2])
        acc_scr[...] = pv if j == 0 else acc_scr[...] + pv
        if j == nkb - 1:
            acc = finish_tile(i)
            not_finite = jnp.where(jnp.abs(acc) < jnp.inf, 0.0, jnp.inf)
            used = jnp.maximum(used, jnp.max(not_finite, axis=0, keepdims=True))
        elif not last:
            acc_scr[...] = acc_scr[...] * jnp.exp2(ref - next_ref)
        if not last:
            ref, bmax = next_ref, next_bmax

    @pl.when(jnp.logical_not(jnp.max(used) <= ATT_HEADROOM))
    def _():
        slots = st_scr.shape[0]
        for n in range(slots - 1):
            st_scr[n] = qk(n)
        m = None
        for n, (i, j) in enumerate(steps):
            if n + slots - 1 < len(steps):
                st_scr[(n + slots - 1) % slots] = qk(n + slots - 1)
            st = st_scr[n % slots]
            blk = jnp.max(st, axis=0, keepdims=True)
            m_new = blk if j == 0 else jnp.maximum(m, blk)
            pv = _dot(vt_ref[j], jnp.exp2(st - m_new).astype(BF16))
            acc_scr[...] = pv if j == 0 else jnp.exp2(m - m_new) * acc_scr[...] + pv
            m = m_new
            if j == nkb - 1:
                finish_tile(i)


def _attention(q, k, vt, *, tq):
    bsz, seq, _ = q.shape
    nkb = vt.shape[0] // bsz
    kb = vt.shape[2]
    head = pl.BlockSpec((None, seq, HEAD_SLAB), lambda b, h: (b, 0, h))
    return pl.pallas_call(
        _attention_kernel,
        grid=(bsz, ATT_HEADS),
        in_specs=[head, head, pl.BlockSpec((nkb, V_ROWS, kb), lambda b, h: (b, h, 0))],
        out_specs=pl.BlockSpec((None, V_DIM, seq), lambda b, h: (b, h, 0)),
        out_shape=jax.ShapeDtypeStruct((bsz, ATT_WIDTH, seq), BF16),
        scratch_shapes=[pltpu.VMEM((2, kb, tq), BF16), pltpu.VMEM((3, kb, tq), F32), pltpu.VMEM((V_ROWS, tq), F32)],
        compiler_params=pltpu.CompilerParams(dimension_semantics=("parallel", "parallel"),
                                             vmem_limit_bytes=VMEM_LIMIT),
        name="attention",
    )(q, k, vt)


_HG_LEVELS = int(np.log2(HG_CHUNK))
_HG_DIAG = _HG_LEVELS
_HG_SUBLANES = 8


@functools.lru_cache(maxsize=None)
def _hgrn_consts():
    c = HG_CHUNK
    t = np.arange(c)[:, None]
    u = np.arange(c)[None, :]
    blocks = [u <= t]
    level = np.full((c, c), -1, np.int32)
    level[np.arange(c), np.arange(c)] = _HG_DIAG
    for li in range(_HG_LEVELS):
        h = (c // 2) >> li
        mid = (t // (2 * h)) * (2 * h) + h - 1
        upper = (t % (2 * h)) >= h
        if 1 < h < _HG_SUBLANES:
            blocks.append((upper & (u > mid) & (u <= t)) | (~upper & (u > t) & (u <= mid)))
        same = (t // (2 * h)) == (u // (2 * h))
        level[same & upper & ((u % (2 * h)) < h)] = li
    m_f = np.concatenate(blocks, 0).astype(np.float32)
    m_b = np.concatenate([blk[::-1, ::-1] for blk in blocks], 0).astype(np.float32)
    return m_f, m_b, level, np.ascontiguousarray(level.T)


def _hgrn_chunk(rows, backward, m_ref, lv_ref, q_ref, kk_ref, g_ref, v_ref, st):
    c = HG_CHUNK
    g = g_ref[rows, :]
    g_hi = g.astype(BF16)
    g_lo = (g - g_hi.astype(F32)).astype(BF16)
    r = _dot(m_ref[...], jnp.concatenate([g_hi, g_lo], axis=1))
    x = r[:, :HG_KEY] + r[:, HG_KEY:]
    b = x[0:c]
    q = q_ref[rows, :].astype(F32)
    kk = kk_ref[rows, :].astype(F32)
    v = v_ref[rows, :]
    lv = lv_ref[...]
    tbit = lax.broadcasted_iota(jnp.int32, (c, HG_KEY), 0)

    tile = _HG_SUBLANES
    lvr = [lv[r * tile:(r + 1) * tile] for r in range(c // tile)]
    diag = jnp.sum(q * kk, axis=-1, keepdims=True)
    srows = [jnp.where(lvr[r] == _HG_DIAG, diag[r * tile:(r + 1) * tile], 0.0) for r in range(c // tile)]
    kk_prev = pltpu.roll(kk, (c - 1) if backward else 1, axis=0)
    near = jnp.sum(q * kk_prev * jnp.exp2(g), axis=-1, keepdims=True)
    small = 1
    for li in range(_HG_LEVELS):
        h = (c // 2) >> li
        if h == 1:
            for r in range(c // tile):
                srows[r] = jnp.where(lvr[r] == li, near[r * tile:(r + 1) * tile], srows[r])
            continue
        if h >= tile:
            parts, qparts, qrows = [], [], []
            for s0 in range(0, c, 2 * h):
                lo, hi = slice(s0, s0 + h), slice(s0 + h, s0 + 2 * h)
                if backward:
                    bm = b[s0 + h:s0 + h + 1]
                    pq, pk = q[lo] * jnp.exp2(b[lo] - bm), kk[hi] * jnp.exp2(bm - b[hi])
                    parts += [pq, pk]
                    qrows += range(s0 // tile, (s0 + h) // tile)
                else:
                    bm = b[s0 + h - 1:s0 + h]
                    pk, pq = kk[lo] * jnp.exp2(bm - b[lo]), q[hi] * jnp.exp2(b[hi] - bm)
                    parts += [pk, pq]
                    qrows += range((s0 + h) // tile, (s0 + 2 * h) // tile)
                qparts.append(pq)
            gram = _dot_nt(jnp.concatenate(qparts, axis=0).astype(BF16),
                           jnp.concatenate(parts, axis=0).astype(BF16))
        else:
            is_query = ((tbit & h) == 0) if backward else ((tbit & h) != 0)
            p = (jnp.where(is_query, q, kk) * jnp.exp2(x[small * c:(small + 1) * c])).astype(BF16)
            small += 1
            gram = _dot_nt(p, p)
            qrows = range(c // tile)
        for i, r in enumerate(qrows):
            srows[r] = jnp.where(lvr[r] == li, gram[i * tile:(i + 1) * tile], srows[r])
    scores = jnp.concatenate(srows, axis=0)

    b_end = b[0:1] if backward else b[c - 1:c]
    o = _dot(scores.astype(BF16), v)
    o = o + _dot_nt((q * jnp.exp2(b)).astype(BF16), st.astype(BF16))
    kd = (kk * jnp.exp2(b_end - b)).astype(BF16)
    st_new = st * jnp.exp2(b_end) + _dot_tn(v, kd)
    return o, st_new


def _hgrn_kernel(mf_ref, mb_ref, lvf_ref, lvb_ref, q_ref, kkf_ref, kkb_ref, gf_ref, gb_ref, v_ref,
                 gate_ref, norm_ref, r_out, acc):
    c = HG_CHUNK
    nc = q_ref.shape[0] // c
    half = nc // 2

    def both(i, st_f, st_b):
        rows_f = pl.ds(pl.multiple_of(i * c, c), c)
        rows_b = pl.ds(pl.multiple_of((nc - 1 - i) * c, c), c)
        o_f, st_f = _hgrn_chunk(rows_f, False, mf_ref, lvf_ref, q_ref, kkf_ref, gf_ref, v_ref, st_f)
        o_b, st_b = _hgrn_chunk(rows_b, True, mb_ref, lvb_ref, q_ref, kkb_ref, gb_ref, v_ref, st_b)
        return rows_f, rows_b, o_f, o_b, st_f, st_b

    def first_touch(i, carry):
        rows_f, rows_b, o_f, o_b, st_f, st_b = both(i, *carry)
        acc[rows_f, :] = o_f
        acc[rows_b, :] = o_b
        return st_f, st_b

    def finish(rows, o):
        tot = acc[rows, :] + o
        r_out[rows, :] = (_rms(tot, norm_ref[...]) * gate_ref[rows, :].astype(F32)).astype(BF16)

    def second_touch(i, carry):
        rows_f, rows_b, o_f, o_b, st_f, st_b = both(i, *carry)
        finish(rows_f, o_f)
        finish(rows_b, o_b)
        return st_f, st_b

    zero = jnp.zeros((HG_VAL, HG_KEY), F32)
    carry = lax.fori_loop(0, half, first_touch, (zero, zero), unroll=8)
    lax.fori_loop(half, nc, second_touch, carry, unroll=8)


def _hgrn(qs, kkf, kkb, gf, gb, hv, gate, norm):
    bsz, seq, _ = qs.shape
    assert seq % (16 * HG_CHUNK) == 0
    m_f, m_b, lv_f, lv_b = _hgrn_consts()
    head = pl.BlockSpec((None, seq, HG_KEY), lambda b, h: (b, 0, h))
    return pl.pallas_call(
        _hgrn_kernel,
        grid=(bsz, HG_HEADS),
        in_specs=[_const_spec(m_f.shape), _const_spec(m_b.shape), _const_spec(lv_f.shape), _const_spec(lv_b.shape),
                  head, head, head, head, head, head, head,
                  pl.BlockSpec((1, HG_VAL), lambda b, h: (0, h))],
        out_specs=head,
        out_shape=jax.ShapeDtypeStruct((bsz, seq, HG_W), BF16),
        scratch_shapes=[pltpu.VMEM((seq, HG_VAL), F32)],
        compiler_params=pltpu.CompilerParams(dimension_semantics=("parallel", "parallel"),
                                             vmem_limit_bytes=VMEM_LIMIT),
        name="hgrn",
    )(jnp.asarray(m_f, BF16), jnp.asarray(m_b, BF16), jnp.asarray(lv_f), jnp.asarray(lv_b),
      qs, kkf, kkb, gf, gb, hv, gate, norm)


_FF_CHUNK = 1024


def _post_kernel(x_ref, at_ref, r_ref, p_ref, na_ref, wo_ref, n1_ref, n2_ref, wup_ref, wdn_ref, n3_ref,
                 wple_ref, wpg_ref, n4_ref, x_out):
    a = _rms(at_ref[...].astype(F32).T, na_ref[...]).astype(BF16)
    mix = _dot(a, wo_ref[:ATT_WIDTH, :]) + _dot(r_ref[...], wo_ref[ATT_WIDTH:, :])
    x = x_ref[...] + _rms(mix, n1_ref[...])
    h = _rms(x, n2_ref[...]).astype(BF16)
    m = jnp.zeros_like(x)
    for c0 in range(0, D_FF, _FF_CHUNK):
        act = jnp.square(jnp.maximum(_dot(h, wup_ref[:, c0:c0 + _FF_CHUNK]), 0.0)).astype(BF16)
        m = m + _dot(act, wdn_ref[c0:c0 + _FF_CHUNK, :])
    x = x + _rms(m, n3_ref[...])
    e = _dot(p_ref[...].astype(BF16), wple_ref[...])
    gate = jax.nn.sigmoid(_dot(x.astype(BF16), wpg_ref[...]))
    x_out[...] = x + _rms(e * gate, n4_ref[...])


def _post(x, at, r, p, na, wo, n1, n2, wup, wdn, n3, wple, wpg, n4, *, tm):
    n = x.shape[0]
    nt = at.shape[2] // tm
    row = lambda w: pl.BlockSpec((tm, w), lambda i: (i, 0))
    at_spec = pl.BlockSpec((None, ATT_WIDTH, tm), lambda i: (i // nt, 0, i % nt))
    vec = _const_spec((1, D_MODEL))
    single = lambda w: pl.BlockSpec(w.shape, lambda i: (0, 0), pipeline_mode=pl.Buffered(1))
    return pl.pallas_call(
        _post_kernel,
        grid=(n // tm,),
        in_specs=[row(D_MODEL), at_spec, row(HG_W), row(PLE_DIM), _const_spec((1, ATT_WIDTH)), single(wo), vec, vec,
                  single(wup), single(wdn), vec, single(wple), single(wpg), vec],
        out_specs=row(D_MODEL),
        out_shape=jax.ShapeDtypeStruct((n, D_MODEL), F32),
        compiler_params=pltpu.CompilerParams(dimension_semantics=("parallel",), vmem_limit_bytes=VMEM_LIMIT),
        name="post",
    )(x, at, r, p, na, wo, n1, n2, wup, wdn, n3, wple, wpg, n4)


def _layout_w_in(w):
    head = w[:, :Q_LORA + KV_LORA + ROPE_DIM]
    rest = w[:, Q_LORA + KV_LORA + ROPE_DIM:]
    pad = jnp.zeros((w.shape[0], LANES - ROPE_DIM), w.dtype)
    return jnp.concatenate([head, pad, rest], axis=1)


def _layout_w_uq(w):
    w = w.reshape(Q_LORA, ATT_HEADS, QK_DIM)
    pad = jnp.zeros((Q_LORA, ATT_HEADS, HEAD_SLAB - QK_DIM), w.dtype)
    return jnp.concatenate([w, pad], axis=-1).reshape(Q_LORA, QK_SLAB)


def _layout_w_ukv(w):
    w = w.reshape(KV_LORA, ATT_HEADS, NOPE_DIM + V_DIM)
    pad = jnp.zeros((KV_LORA, ATT_HEADS, HEAD_SLAB - NOPE_DIM), w.dtype)
    k = jnp.concatenate([w[..., :NOPE_DIM], pad], axis=-1).reshape(KV_LORA, QK_SLAB)
    vpad = jnp.zeros((KV_LORA, ATT_HEADS, V_ROWS - V_DIM), w.dtype)
    v = jnp.concatenate([w[..., NOPE_DIM:], vpad], axis=-1).reshape(KV_LORA, VT_ROWS)
    return jnp.concatenate([k, v], axis=1)


def _value_ones():
    o = np.zeros((ATT_HEADS, V_ROWS), np.float32)
    o[:, V_DIM:] = 1.0
    return o.reshape(1, VT_ROWS)


def _rope_tables(seq):
    inv = 1.0 / (ROPE_THETA ** (jnp.arange(0, ROPE_DIM, 2, dtype=F32) / ROPE_DIM))
    ang = jnp.arange(seq, dtype=F32)[:, None] * inv[None, :]
    cos, sin = jnp.cos(ang), jnp.sin(ang)
    scale = jnp.asarray(QK_DIM ** -0.5 * LOG2E, F32)
    z = lambda w: jnp.zeros((seq, w), F32)
    zs = jnp.zeros_like(sin)
    cq = jnp.concatenate([jnp.ones((seq, NOPE_DIM), F32), cos, cos, z(HEAD_SLAB - QK_DIM)], axis=1) * scale
    sqa = jnp.concatenate([z(NOPE_DIM), -sin, zs, z(HEAD_SLAB - QK_DIM)], axis=1) * scale
    sqb = jnp.concatenate([z(NOPE_DIM), zs, sin, z(HEAD_SLAB - QK_DIM)], axis=1) * scale
    ck = jnp.concatenate([cos, cos, z(LANES - ROPE_DIM)], axis=1)
    ska = jnp.concatenate([-sin, zs, z(LANES - ROPE_DIM)], axis=1)
    skb = jnp.concatenate([zs, sin, z(LANES - ROPE_DIM)], axis=1)
    return cq, sqa, sqb, ck, ska, skb


def _trunk(x, p, lbs, norm_mix_pre, w_in, q_norm, w_uq, kv_norm, w_ukv, att_out_norm, hg_out_norm, w_o,
           norm_mix_post, norm_mlp_pre, w_up, w_down, norm_mlp_post, w_ple, w_ple_gate, norm_ple):
    bsz, seq, _ = x.shape
    n = bsz * seq
    tm = min(512, seq)
    tq = min(1024, seq)
    tables = _rope_tables(seq)
    ones = jnp.asarray(_value_ones())
    row = lambda v: v.reshape(1, -1).astype(F32)
    xf = x.reshape(n, D_MODEL)
    for i in range(DEPTH):
        q, k, vt, hq, kkf, kkb, gf, gb, hv, gate = _in_proj(
            xf, row(norm_mix_pre[i]), _layout_w_in(w_in[i]).astype(BF16), row(q_norm[i]), row(kv_norm[i]),
            _layout_w_uq(w_uq[i]).astype(BF16), _layout_w_ukv(w_ukv[i]).astype(BF16), ones,
            tables, row(lbs[0, i]), row(lbs[1, i]), seq=seq, tm=tm)
        b3 = lambda t: t.reshape(bsz, seq, t.shape[-1])
        at = _attention(b3(q), b3(k), vt, tq=tq)
        r = _hgrn(b3(hq), b3(kkf), b3(kkb), b3(gf), b3(gb), b3(hv), b3(gate), row(hg_out_norm[i]))
        xf = _post(xf, at, r.reshape(n, HG_W), p[i].reshape(n, PLE_DIM), row(att_out_norm[i]),
                   w_o[i].astype(BF16), row(norm_mix_post[i]), row(norm_mlp_pre[i]), w_up[i].astype(BF16),
                   w_down[i].astype(BF16), row(norm_mlp_post[i]), w_ple[i].astype(BF16),
                   w_ple_gate[i].astype(BF16), row(norm_ple[i]), tm=min(2 * tm, seq))
    return xf.reshape(bsz, seq, D_MODEL)


def kernel(x_prompt, x_sample, p_prompt, p_sample, norm_mix_pre, w_in, q_norm, w_uq, kv_norm, w_ukv, att_out_norm, hg_lb, hg_out_norm, w_o, norm_mix_post, norm_mlp_pre, w_up, w_down, norm_mlp_post, w_ple, w_ple_gate, norm_ple):
    lbs = jnp.cumsum(jax.nn.softmax(hg_lb.astype(F32), axis=1), axis=1)
    lbs = lbs - lbs[:, :1]
    ws = (norm_mix_pre, w_in, q_norm, w_uq, kv_norm, w_ukv, att_out_norm, hg_out_norm, w_o, norm_mix_post,
          norm_mlp_pre, w_up, w_down, norm_mlp_post, w_ple, w_ple_gate, norm_ple)
    return (_trunk(x_prompt, p_prompt, lbs, *ws), _trunk(x_sample, p_sample, lbs, *ws))
```

```python
import functools

import numpy as np
import jax
import jax.numpy as jnp
from jax import lax
from jax.experimental import pallas as pl
from jax.experimental.pallas import tpu as pltpu

D_MODEL = 1024
DEPTH = 4
ATT_HEADS = 8
Q_LORA = 256
KV_LORA = 128
NOPE_DIM = 64
ROPE_DIM = 32
V_DIM = 64
QK_DIM = NOPE_DIM + ROPE_DIM
ATT_WIDTH = ATT_HEADS * V_DIM
HG_HEADS = 4
HG_KEY = 128
HG_VAL = 128
HG_W = HG_HEADS * HG_KEY
D_FF = 4 * D_MODEL
PLE_DIM = 256
ROPE_THETA = 10000.0
EPS = 1e-6

LANES = 128
HEAD_SLAB = LANES
QK_SLAB = ATT_HEADS * HEAD_SLAB
BF16_SUBLANES = 16
V_ROWS = V_DIM + BF16_SUBLANES
VT_ROWS = ATT_HEADS * V_ROWS
LOG2E = 1.4426950408889634
HG_CHUNK = 128
ATT_HEADROOM = 96.0
VMEM_LIMIT = 56 * 1024 * 1024

F32 = jnp.float32
BF16 = jnp.bfloat16

_C_CQ = 0
_C_CKV = _C_CQ + Q_LORA
_C_KR = _C_CKV + KV_LORA
_C_HQ = _C_KR + LANES
_C_HFF = _C_HQ + HG_W
_C_HFB = _C_HFF + HG_W
_C_HI = _C_HFB + HG_W
_C_HG = _C_HI + HG_W
_IN_COLS_P = _C_HG + HG_W


def _rms(x, gain):
    return x * lax.rsqrt(jnp.mean(x * x, axis=-1, keepdims=True) + EPS) * gain


def _dot(a, b):
    return jnp.dot(a, b, preferred_element_type=F32)


def _dot_nt(a, b):
    return lax.dot_general(a, b, (((1,), (1,)), ((), ())), preferred_element_type=F32)


def _dot_tn(a, b):
    return lax.dot_general(a, b, (((0,), (0,)), ((), ())), preferred_element_type=F32)


def _const_spec(shape):
    return pl.BlockSpec(shape, lambda *_: (0,) * len(shape))


def _rope(x, cos, sin_lo, sin_hi):
    half = ROPE_DIM // 2
    return x * cos + pltpu.roll(x, LANES - half, axis=1) * sin_lo + pltpu.roll(x, half, axis=1) * sin_hi


def _in_proj_kernel(x_ref, g_ref, win_ref, qn_ref, kvn_ref, wuq_ref, wukv_ref, ones_ref,
                    cq_ref, sqa_ref, sqb_ref, ck_ref, ska_ref, skb_ref, lbf_ref, lbb_ref,
                    q_out, k_out, v_out, hq_out, kkf_out, kkb_out, gf_out, gb_out, hv_out, gate_out):
    h = _rms(x_ref[...], g_ref[...]).astype(BF16)
    z = _dot(h, win_ref[...])

    qn = _rms(z[:, _C_CQ:_C_CKV], qn_ref[...]).astype(BF16)
    q1 = _dot(qn, wuq_ref[...])
    cq, sqa, sqb = cq_ref[...], sqa_ref[...], sqb_ref[...]
    for hd in range(ATT_HEADS):
        slab = slice(hd * HEAD_SLAB, (hd + 1) * HEAD_SLAB)
        q_out[:, slab] = _rope(q1[:, slab], cq, sqa, sqb).astype(BF16)

    kvn = _rms(z[:, _C_CKV:_C_KR], kvn_ref[...]).astype(BF16)
    kv2 = _dot(kvn, wukv_ref[...])
    kr = _rope(z[:, _C_KR:_C_HQ], ck_ref[...], ska_ref[...], skb_ref[...])
    kr = pltpu.roll(kr, NOPE_DIM, axis=1)
    for hd in range(ATT_HEADS):
        slab = slice(hd * HEAD_SLAB, (hd + 1) * HEAD_SLAB)
        k_out[:, slab] = (kv2[:, slab] + kr).astype(BF16)
    v_out[...] = (kv2[:, QK_SLAB:] + ones_ref[...]).T.astype(BF16)

    hq = z[:, _C_HQ:_C_HFF]
    hq_out[...] = (hq * jax.nn.sigmoid(hq)).astype(BF16)
    for src, lb_ref, kk_o, g_o in ((_C_HFF, lbf_ref, kkf_out, gf_out), (_C_HFB, lbb_ref, kkb_out, gb_out)):
        lb = lb_ref[...]
        f = lb + (1.0 - lb) * jax.nn.sigmoid(z[:, src:src + HG_W])
        g_o[...] = jnp.log2(f)
        kk_o[...] = (1.0 - f).astype(BF16)
    hv_out[...] = z[:, _C_HI:_C_HG].astype(BF16)
    hg = z[:, _C_HG:_IN_COLS_P]
    gate_out[...] = (hg * jax.nn.sigmoid(hg)).astype(BF16)


def _in_proj(x, gain, win, qn, kvn, wuq, wukv, ones, tables, lbf, lbb, *, seq, tm):
    n = x.shape[0]
    nt = seq // tm
    row = lambda w: pl.BlockSpec((tm, w), lambda i: (i, 0))
    tab = pl.BlockSpec((tm, LANES), lambda i: (i % nt, 0))
    outs = [(QK_SLAB, BF16), (QK_SLAB, BF16), None, (HG_W, BF16), (HG_W, BF16), (HG_W, BF16),
            (HG_W, F32), (HG_W, F32), (HG_W, BF16), (HG_W, BF16)]
    vt_spec = pl.BlockSpec((None, VT_ROWS, tm), lambda i: (i, 0, 0))
    vt_shape = jax.ShapeDtypeStruct((n // tm, VT_ROWS, tm), BF16)
    return pl.pallas_call(
        _in_proj_kernel,
        grid=(n // tm,),
        in_specs=[row(D_MODEL), _const_spec((1, D_MODEL)), _const_spec(win.shape),
                  _const_spec((1, Q_LORA)), _const_spec((1, KV_LORA)), _const_spec(wuq.shape),
                  _const_spec(wukv.shape), _const_spec(ones.shape)] + [tab] * len(tables) + [
                  _const_spec((1, HG_W)), _const_spec((1, HG_W))],
        out_specs=[vt_spec if o is None else row(o[0]) for o in outs],
        out_shape=[vt_shape if o is None else jax.ShapeDtypeStruct((n, o[0]), o[1]) for o in outs],
        compiler_params=pltpu.CompilerParams(dimension_semantics=("parallel",), vmem_limit_bytes=VMEM_LIMIT),
        name="in_proj",
    )(x, gain, win, qn, kvn, wuq, wukv, ones, *tables, lbf, lbb)


def _attention_kernel(q_ref, k_ref, vt_ref, ot_out, p_scr, st_scr, acc_scr):
    nkb, _, kb = vt_ref.shape
    tq = p_scr.shape[2]
    steps = [(i, j) for i in range(q_ref.shape[0] // tq) for j in range(nkb)]

    def qk(n):
        i, j = steps[n]
        return _dot_nt(k_ref[j * kb:(j + 1) * kb, :], q_ref[i * tq:(i + 1) * tq, :])

    def finish_tile(i):
        acc = acc_scr[...]
        ot_out[:, i * tq:(i + 1) * tq] = (acc[:V_DIM] / acc[V_DIM:V_DIM + 1]).astype(BF16)
        return acc

    def stream(n, ref):
        st = qk(n)
        p_scr[n % 2] = jnp.exp2(st - ref).astype(BF16)
        return jnp.max(st, axis=0, keepdims=True)

    zero = jnp.zeros((1, tq), F32)
    ref = zero
    bmax = stream(0, ref)
    used = jnp.abs(bmax)
    for n, (i, j) in enumerate(steps):
        last = n + 1 == len(steps)
        if not last:
            first_of_tile = steps[n + 1][1] == 0
            next_ref = zero if first_of_tile else jnp.maximum(ref, bmax)
            next_bmax = stream(n + 1, next_ref)
            used = jnp.maximum(used, jnp.abs(next_bmax) if first_of_tile else next_bmax - next_ref)
        pv = _dot(vt_ref[j], p_scr[n % 2])
        acc_scr[...] = pv if j == 0 else acc_scr[...] + pv
        if j == nkb - 1:
            acc = finish_tile(i)
            not_finite = jnp.where(jnp.abs(acc) < jnp.inf, 0.0, jnp.inf)
            used = jnp.maximum(used, jnp.max(not_finite, axis=0, keepdims=True))
        elif not last:
            acc_scr[...] = acc_scr[...] * jnp.exp2(ref - next_ref)
        if not last:
            ref, bmax = next_ref, next_bmax

    @pl.when(jnp.logical_not(jnp.max(used) <= ATT_HEADROOM))
    def _():
        slots = st_scr.shape[0]
        for n in range(slots - 1):
            st_scr[n] = qk(n)
        m = None
        for n, (i, j) in enumerate(steps):
            if n + slots - 1 < len(steps):
                st_scr[(n + slots - 1) % slots] = qk(n + slots - 1)
            st = st_scr[n % slots]
            blk = jnp.max(st, axis=0, keepdims=True)
            m_new = blk if j == 0 else jnp.maximum(m, blk)
            pv = _dot(vt_ref[j], jnp.exp2(st - m_new).astype(BF16))
            acc_scr[...] = pv if j == 0 else jnp.exp2(m - m_new) * acc_scr[...] + pv
            m = m_new
            if j == nkb - 1:
                finish_tile(i)


def _attention(q, k, vt, *, tq):
    bsz, seq, _ = q.shape
    nkb = vt.shape[0] // bsz
    kb = vt.shape[2]
    head = pl.BlockSpec((None, seq, HEAD_SLAB), lambda b, h: (b, 0, h))
    return pl.pallas_call(
        _attention_kernel,
        grid=(bsz, ATT_HEADS),
        in_specs=[head, head, pl.BlockSpec((nkb, V_ROWS, kb), lambda b, h: (b, h, 0))],
        out_specs=pl.BlockSpec((None, V_DIM, seq), lambda b, h: (b, h, 0)),
        out_shape=jax.ShapeDtypeStruct((bsz, ATT_WIDTH, seq), BF16),
        scratch_shapes=[pltpu.VMEM((2, kb, tq), BF16), pltpu.VMEM((3, kb, tq), F32), pltpu.VMEM((V_ROWS, tq), F32)],
        compiler_params=pltpu.CompilerParams(dimension_semantics=("parallel", "parallel"),
                                             vmem_limit_bytes=VMEM_LIMIT),
        name="attention",
    )(q, k, vt)


_HG_LEVELS = int(np.log2(HG_CHUNK))
_HG_DIAG = _HG_LEVELS
_HG_SUBLANES = 8


@functools.lru_cache(maxsize=None)
def _hgrn_consts():
    c = HG_CHUNK
    t = np.arange(c)[:, None]
    u = np.arange(c)[None, :]
    blocks = [u <= t]
    level = np.full((c, c), -1, np.int32)
    level[np.arange(c), np.arange(c)] = _HG_DIAG
    for li in range(_HG_LEVELS):
        h = (c // 2) >> li
        mid = (t // (2 * h)) * (2 * h) + h - 1
        upper = (t % (2 * h)) >= h
        if 1 < h < _HG_SUBLANES:
            blocks.append((upper & (u > mid) & (u <= t)) | (~upper & (u > t) & (u <= mid)))
        same = (t // (2 * h)) == (u // (2 * h))
        level[same & upper & ((u % (2 * h)) < h)] = li
    m_f = np.concatenate(blocks, 0).astype(np.float32)
    m_b = np.concatenate([blk[::-1, ::-1] for blk in blocks], 0).astype(np.float32)
    return m_f, m_b, level, np.ascontiguousarray(level.T)


def _hgrn_chunk(rows, backward, m_ref, lv_ref, q_ref, kk_ref, g_ref, v_ref, st):
    c = HG_CHUNK
    g = g_ref[rows, :]
    g_hi = g.astype(BF16)
    g_lo = (g - g_hi.astype(F32)).astype(BF16)
    r = _dot(m_ref[...], jnp.concatenate([g_hi, g_lo], axis=1))
    x = r[:, :HG_KEY] + r[:, HG_KEY:]
    b = x[0:c]
    q = q_ref[rows, :].astype(F32)
    kk = kk_ref[rows, :].astype(F32)
    v = v_ref[rows, :]
    lv = lv_ref[...]
    tbit = lax.broadcasted_iota(jnp.int32, (c, HG_KEY), 0)

    tile = _HG_SUBLANES
    lvr = [lv[r * tile:(r + 1) * tile] for r in range(c // tile)]
    diag = jnp.sum(q * kk, axis=-1, keepdims=True)
    srows = [jnp.where(lvr[r] == _HG_DIAG, diag[r * tile:(r + 1) * tile], 0.0) for r in range(c // tile)]
    kk_prev = pltpu.roll(kk, (c - 1) if backward else 1, axis=0)
    near = jnp.sum(q * kk_prev * jnp.exp2(g), axis=-1, keepdims=True)
    small = 1
    for li in range(_HG_LEVELS):
        h = (c // 2) >> li
        if h == 1:
            for r in range(c // tile):
                srows[r] = jnp.where(lvr[r] == li, near[r * tile:(r + 1) * tile], srows[r])
            continue
        if h >= tile:
            parts, qparts, qrows = [], [], []
            for s0 in range(0, c, 2 * h):
                lo, hi = slice(s0, s0 + h), slice(s0 + h, s0 + 2 * h)
                if backward:
                    bm = b[s0 + h:s0 + h + 1]
                    pq, pk = q[lo] * jnp.exp2(b[lo] - bm), kk[hi] * jnp.exp2(bm - b[hi])
                    parts += [pq, pk]
                    qrows += range(s0 // tile, (s0 + h) // tile)
                else:
                    bm = b[s0 + h - 1:s0 + h]
                    pk, pq = kk[lo] * jnp.exp2(bm - b[lo]), q[hi] * jnp.exp2(b[hi] - bm)
                    parts += [pk, pq]
                    qrows += range((s0 + h) // tile, (s0 + 2 * h) // tile)
                qparts.append(pq)
            gram = _dot_nt(jnp.concatenate(qparts, axis=0).astype(BF16),
                           jnp.concatenate(parts, axis=0).astype(BF16))
        else:
            is_query = ((tbit & h) == 0) if backward else ((tbit & h) != 0)
            p = (jnp.where(is_query, q, kk) * jnp.exp2(x[small * c:(small + 1) * c])).astype(BF16)
            small += 1
            gram = _dot_nt(p, p)
            qrows = range(c // tile)
        for i, r in enumerate(qrows):
            srows[r] = jnp.where(lvr[r] == li, gram[i * tile:(i + 1) * tile], srows[r])
    scores = jnp.concatenate(srows, axis=0)

    b_end = b[0:1] if backward else b[c - 1:c]
    o = _dot(scores.astype(BF16), v)
    o = o + _dot_nt((q * jnp.exp2(b)).astype(BF16), st.astype(BF16))
    kd = (kk * jnp.exp2(b_end - b)).astype(BF16)
    st_new = st * jnp.exp2(b_end) + _dot_tn(v, kd)
    return o, st_new


def _hgrn_kernel(mf_ref, mb_ref, lvf_ref, lvb_ref, q_ref, kkf_ref, kkb_ref, gf_ref, gb_ref, v_ref,
                 gate_ref, norm_ref, r_out, acc):
    c = HG_CHUNK
    nc = q_ref.shape[0] // c
    half = nc // 2

    def both(i, st_f, st_b):
        rows_f = pl.ds(pl.multiple_of(i * c, c), c)
        rows_b = pl.ds(pl.multiple_of((nc - 1 - i) * c, c), c)
        o_f, st_f = _hgrn_chunk(rows_f, False, mf_ref, lvf_ref, q_ref, kkf_ref, gf_ref, v_ref, st_f)
        o_b, st_b = _hgrn_chunk(rows_b, True, mb_ref, lvb_ref, q_ref, kkb_ref, gb_ref, v_ref, st_b)
        return rows_f, rows_b, o_f, o_b, st_f, st_b

    def first_touch(i, carry):
        rows_f, rows_b, o_f, o_b, st_f, st_b = both(i, *carry)
        acc[rows_f, :] = o_f
        acc[rows_b, :] = o_b
        return st_f, st_b

    def finish(rows, o):
        tot = acc[rows, :] + o
        r_out[rows, :] = (_rms(tot, norm_ref[...]) * gate_ref[rows, :].astype(F32)).astype(BF16)

    def second_touch(i, carry):
        rows_f, rows_b, o_f, o_b, st_f, st_b = both(i, *carry)
        finish(rows_f, o_f)
        finish(rows_b, o_b)
        return st_f, st_b

    zero = jnp.zeros((HG_VAL, HG_KEY), F32)
    carry = lax.fori_loop(0, half, first_touch, (zero, zero), unroll=8)
    lax.fori_loop(half, nc, second_touch, carry, unroll=8)


def _hgrn(qs, kkf, kkb, gf, gb, hv, gate, norm):
    bsz, seq, _ = qs.shape
    assert seq % (16 * HG_CHUNK) == 0
    m_f, m_b, lv_f, lv_b = _hgrn_consts()
    head = pl.BlockSpec((None, seq, HG_KEY), lambda b, h: (b, 0, h))
    return pl.pallas_call(
        _hgrn_kernel,
        grid=(bsz, HG_HEADS),
        in_specs=[_const_spec(m_f.shape), _const_spec(m_b.shape), _const_spec(lv_f.shape), _const_spec(lv_b.shape),
                  head, head, head, head, head, head, head,
                  pl.BlockSpec((1, HG_VAL), lambda b, h: (0, h))],
        out_specs=head,
        out_shape=jax.ShapeDtypeStruct((bsz, seq, HG_W), BF16),
        scratch_shapes=[pltpu.VMEM((seq, HG_VAL), F32)],
        compiler_params=pltpu.CompilerParams(dimension_semantics=("parallel", "parallel"),
                                             vmem_limit_bytes=VMEM_LIMIT),
        name="hgrn",
    )(jnp.asarray(m_f, BF16), jnp.asarray(m_b, BF16), jnp.asarray(lv_f), jnp.asarray(lv_b),
      qs, kkf, kkb, gf, gb, hv, gate, norm)


_FF_CHUNK = 1024


def _post_kernel(x_ref, at_ref, r_ref, p_ref, na_ref, wo_ref, n1_ref, n2_ref, wup_ref, wdn_ref, n3_ref,
                 wple_ref, wpg_ref, n4_ref, x_out):
    a = _rms(at_ref[...].astype(F32).T, na_ref[...]).astype(BF16)
    mix = _dot(a, wo_ref[:ATT_WIDTH, :]) + _dot(r_ref[...], wo_ref[ATT_WIDTH:, :])
    x = x_ref[...] + _rms(mix, n1_ref[...])
    h = _rms(x, n2_ref[...]).astype(BF16)
    m = jnp.zeros_like(x)
    for c0 in range(0, D_FF, _FF_CHUNK):
        act = jnp.square(jnp.maximum(_dot(h, wup_ref[:, c0:c0 + _FF_CHUNK]), 0.0)).astype(BF16)
        m = m + _dot(act, wdn_ref[c0:c0 + _FF_CHUNK, :])
    x = x + _rms(m, n3_ref[...])
    e = _dot(p_ref[...].astype(BF16), wple_ref[...])
    gate = jax.nn.sigmoid(_dot(x.astype(BF16), wpg_ref[...]))
    x_out[...] = x + _rms(e * gate, n4_ref[...])


def _post(x, at, r, p, layer, na, wo, n1, n2, wup, wdn, n3, wple, wpg, n4, *, tm):
    n = x.shape[0]
    nt = at.shape[2] // tm
    row = lambda w: pl.BlockSpec((tm, w), lambda i: (i, 0))
    at_spec = pl.BlockSpec((None, ATT_WIDTH, tm), lambda i: (i // nt, 0, i % nt))
    vec = _const_spec((1, D_MODEL))
    single = lambda w: pl.BlockSpec(w.shape, lambda i: (0, 0), pipeline_mode=pl.Buffered(1))
    return pl.pallas_call(
        _post_kernel,
        grid=(n // tm,),
        in_specs=[row(D_MODEL), at_spec, row(HG_W), pl.BlockSpec((None, tm, PLE_DIM), lambda i: (layer, i, 0)),
                  _const_spec((1, ATT_WIDTH)), single(wo), vec, vec,
                  single(wup), single(wdn), vec, single(wple), single(wpg), vec],
        out_specs=row(D_MODEL),
        out_shape=jax.ShapeDtypeStruct((n, D_MODEL), F32),
        compiler_params=pltpu.CompilerParams(dimension_semantics=("parallel",), vmem_limit_bytes=VMEM_LIMIT),
        name="post",
    )(x, at, r, p, na, wo, n1, n2, wup, wdn, n3, wple, wpg, n4)


def _layout_w_in(w):
    head = w[:, :Q_LORA + KV_LORA + ROPE_DIM]
    rest = w[:, Q_LORA + KV_LORA + ROPE_DIM:]
    pad = jnp.zeros((w.shape[0], LANES - ROPE_DIM), w.dtype)
    return jnp.concatenate([head, pad, rest], axis=1)


def _layout_w_uq(w):
    w = w.reshape(Q_LORA, ATT_HEADS, QK_DIM)
    pad = jnp.zeros((Q_LORA, ATT_HEADS, HEAD_SLAB - QK_DIM), w.dtype)
    return jnp.concatenate([w, pad], axis=-1).reshape(Q_LORA, QK_SLAB)


def _layout_w_ukv(w):
    w = w.reshape(KV_LORA, ATT_HEADS, NOPE_DIM + V_DIM)
    pad = jnp.zeros((KV_LORA, ATT_HEADS, HEAD_SLAB - NOPE_DIM), w.dtype)
    k = jnp.concatenate([w[..., :NOPE_DIM], pad], axis=-1).reshape(KV_LORA, QK_SLAB)
    vpad = jnp.zeros((KV_LORA, ATT_HEADS, V_ROWS - V_DIM), w.dtype)
    v = jnp.concatenate([w[..., NOPE_DIM:], vpad], axis=-1).reshape(KV_LORA, VT_ROWS)
    return jnp.concatenate([k, v], axis=1)


def _value_ones():
    o = np.zeros((ATT_HEADS, V_ROWS), np.float32)
    o[:, V_DIM:] = 1.0
    return o.reshape(1, VT_ROWS)


def _rope_tables(seq):
    inv = 1.0 / (ROPE_THETA ** (jnp.arange(0, ROPE_DIM, 2, dtype=F32) / ROPE_DIM))
    ang = jnp.arange(seq, dtype=F32)[:, None] * inv[None, :]
    cos, sin = jnp.cos(ang), jnp.sin(ang)
    scale = jnp.asarray(QK_DIM ** -0.5 * LOG2E, F32)
    z = lambda w: jnp.zeros((seq, w), F32)
    zs = jnp.zeros_like(sin)
    cq = jnp.concatenate([jnp.ones((seq, NOPE_DIM), F32), cos, cos, z(HEAD_SLAB - QK_DIM)], axis=1) * scale
    sqa = jnp.concatenate([z(NOPE_DIM), -sin, zs, z(HEAD_SLAB - QK_DIM)], axis=1) * scale
    sqb = jnp.concatenate([z(NOPE_DIM), zs, sin, z(HEAD_SLAB - QK_DIM)], axis=1) * scale
    ck = jnp.concatenate([cos, cos, z(LANES - ROPE_DIM)], axis=1)
    ska = jnp.concatenate([-sin, zs, z(LANES - ROPE_DIM)], axis=1)
    skb = jnp.concatenate([zs, sin, z(LANES - ROPE_DIM)], axis=1)
    return cq, sqa, sqb, ck, ska, skb


def _trunk(x, p, lbs, norm_mix_pre, w_in, q_norm, w_uq, kv_norm, w_ukv, att_out_norm, hg_out_norm, w_o,
           norm_mix_post, norm_mlp_pre, w_up, w_down, norm_mlp_post, w_ple, w_ple_gate, norm_ple):
    bsz, seq, _ = x.shape
    n = bsz * seq
    tm = min(512, seq)
    tq = min(512, seq)
    tables = _rope_tables(seq)
    ones = jnp.asarray(_value_ones())
    row = lambda v: v.reshape(1, -1).astype(F32)
    xf = x.reshape(n, D_MODEL)
    for i in range(DEPTH):
        q, k, vt, hq, kkf, kkb, gf, gb, hv, gate = _in_proj(
            xf, row(norm_mix_pre[i]), _layout_w_in(w_in[i]).astype(BF16), row(q_norm[i]), row(kv_norm[i]),
            _layout_w_uq(w_uq[i]).astype(BF16), _layout_w_ukv(w_ukv[i]).astype(BF16), ones,
            tables, row(lbs[0, i]), row(lbs[1, i]), seq=seq, tm=tm)
        b3 = lambda t: t.reshape(bsz, seq, t.shape[-1])
        at = _attention(b3(q), b3(k), vt, tq=tq)
        r = _hgrn(b3(hq), b3(kkf), b3(kkb), b3(gf), b3(gb), b3(hv), b3(gate), row(hg_out_norm[i]))
        xf = _post(xf, at, r.reshape(n, HG_W), p.reshape(DEPTH, n, PLE_DIM), i, row(att_out_norm[i]),
                   w_o[i].astype(BF16), row(norm_mix_post[i]), row(norm_mlp_pre[i]), w_up[i].astype(BF16),
                   w_down[i].astype(BF16), row(norm_mlp_post[i]), w_ple[i].astype(BF16),
                   w_ple_gate[i].astype(BF16), row(norm_ple[i]), tm=min(2 * tm, seq))
    return xf.reshape(bsz, seq, D_MODEL)


def kernel(x_prompt, x_sample, p_prompt, p_sample, norm_mix_pre, w_in, q_norm, w_uq, kv_norm, w_ukv, att_out_norm, hg_lb, hg_out_norm, w_o, norm_mix_post, norm_mlp_pre, w_up, w_down, norm_mlp_post, w_ple, w_ple_gate, norm_ple):
    lbs = jnp.cumsum(jax.nn.softmax(hg_lb.astype(F32), axis=1), axis=1)
    lbs = lbs - lbs[:, :1]
    ws = (norm_mix_pre, w_in, q_norm, w_uq, kv_norm, w_ukv, att_out_norm, hg_out_norm, w_o, norm_mix_post,
          norm_mlp_pre, w_up, w_down, norm_mlp_post, w_ple, w_ple_gate, norm_ple)
    return (_trunk(x_prompt, p_prompt, lbs, *ws), _trunk(x_sample, p_sample, lbs, *ws))
```

```python
import functools

import numpy as np
import jax
import jax.numpy as jnp
from jax import lax
from jax.experimental import pallas as pl
from jax.experimental.pallas import tpu as pltpu

D_MODEL = 1024
DEPTH = 4
ATT_HEADS = 8
Q_LORA = 256
KV_LORA = 128
NOPE_DIM = 64
ROPE_DIM = 32
V_DIM = 64
QK_DIM = NOPE_DIM + ROPE_DIM
ATT_WIDTH = ATT_HEADS * V_DIM
HG_HEADS = 4
HG_KEY = 128
HG_VAL = 128
HG_W = HG_HEADS * HG_KEY
D_FF = 4 * D_MODEL
PLE_DIM = 256
ROPE_THETA = 10000.0
EPS = 1e-6

LANES = 128
HEAD_SLAB = LANES
QK_SLAB = ATT_HEADS * HEAD_SLAB
BF16_SUBLANES = 16
V_ROWS = V_DIM + BF16_SUBLANES
VT_ROWS = ATT_HEADS * V_ROWS
LOG2E = 1.4426950408889634
HG_CHUNK = 128
ATT_HEADROOM = 96.0
VMEM_LIMIT = 56 * 1024 * 1024

F32 = jnp.float32
BF16 = jnp.bfloat16

_C_CQ = 0
_C_CKV = _C_CQ + Q_LORA
_C_KR = _C_CKV + KV_LORA
_C_HQ = _C_KR + LANES
_C_HFF = _C_HQ + HG_W
_C_HFB = _C_HFF + HG_W
_C_HI = _C_HFB + HG_W
_C_HG = _C_HI + HG_W
_IN_COLS_P = _C_HG + HG_W


def _rms(x, gain):
    return x * lax.rsqrt(jnp.mean(x * x, axis=-1, keepdims=True) + EPS) * gain


def _dot(a, b):
    return jnp.dot(a, b, preferred_element_type=F32)


def _dot_nt(a, b):
    return lax.dot_general(a, b, (((1,), (1,)), ((), ())), preferred_element_type=F32)


def _dot_tn(a, b):
    return lax.dot_general(a, b, (((0,), (0,)), ((), ())), preferred_element_type=F32)


def _const_spec(shape):
    return pl.BlockSpec(shape, lambda *_: (0,) * len(shape))


def _rope(x, cos, sin_lo, sin_hi):
    half = ROPE_DIM // 2
    return x * cos + pltpu.roll(x, LANES - half, axis=1) * sin_lo + pltpu.roll(x, half, axis=1) * sin_hi


def _in_proj_kernel(x_ref, g_ref, win_ref, qn_ref, kvn_ref, wuq_ref, wukv_ref, ones_ref,
                    cq_ref, sqa_ref, sqb_ref, ck_ref, ska_ref, skb_ref, lbf_ref, lbb_ref,
                    q_out, k_out, v_out, hq_out, kkf_out, kkb_out, gf_out, gb_out, hv_out, gate_out):
    h = _rms(x_ref[...], g_ref[...]).astype(BF16)
    z = _dot(h, win_ref[...])

    qn = _rms(z[:, _C_CQ:_C_CKV], qn_ref[...]).astype(BF16)
    q1 = _dot(qn, wuq_ref[...])
    cq, sqa, sqb = cq_ref[...], sqa_ref[...], sqb_ref[...]
    for hd in range(ATT_HEADS):
        slab = slice(hd * HEAD_SLAB, (hd + 1) * HEAD_SLAB)
        q_out[:, slab] = _rope(q1[:, slab], cq, sqa, sqb).astype(BF16)

    kvn = _rms(z[:, _C_CKV:_C_KR], kvn_ref[...]).astype(BF16)
    kv2 = _dot(kvn, wukv_ref[...])
    kr = _rope(z[:, _C_KR:_C_HQ], ck_ref[...], ska_ref[...], skb_ref[...])
    kr = pltpu.roll(kr, NOPE_DIM, axis=1)
    for hd in range(ATT_HEADS):
        slab = slice(hd * HEAD_SLAB, (hd + 1) * HEAD_SLAB)
        k_out[:, slab] = (kv2[:, slab] + kr).astype(BF16)
    v_out[...] = (kv2[:, QK_SLAB:] + ones_ref[...]).T.astype(BF16)

    hq = z[:, _C_HQ:_C_HFF]
    hq_out[...] = (hq * jax.nn.sigmoid(hq)).astype(BF16)
    for src, lb_ref, kk_o, g_o in ((_C_HFF, lbf_ref, kkf_out, gf_out), (_C_HFB, lbb_ref, kkb_out, gb_out)):
        lb = lb_ref[...]
        f = lb + (1.0 - lb) * jax.nn.sigmoid(z[:, src:src + HG_W])
        g_o[...] = jnp.log2(f)
        kk_o[...] = (1.0 - f).astype(BF16)
    hv_out[...] = z[:, _C_HI:_C_HG].astype(BF16)
    hg = z[:, _C_HG:_IN_COLS_P]
    gate_out[...] = (hg * jax.nn.sigmoid(hg)).astype(BF16)


def _in_proj(x, gain, win, qn, kvn, wuq, wukv, ones, tables, lbf, lbb, *, seq, tm):
    n = x.shape[0]
    nt = seq // tm
    row = lambda w: pl.BlockSpec((tm, w), lambda i: (i, 0))
    tab = pl.BlockSpec((tm, LANES), lambda i: (i % nt, 0))
    outs = [(QK_SLAB, BF16), (QK_SLAB, BF16), None, (HG_W, BF16), (HG_W, BF16), (HG_W, BF16),
            (HG_W, F32), (HG_W, F32), (HG_W, BF16), (HG_W, BF16)]
    vt_spec = pl.BlockSpec((None, VT_ROWS, tm), lambda i: (i, 0, 0))
    vt_shape = jax.ShapeDtypeStruct((n // tm, VT_ROWS, tm), BF16)
    return pl.pallas_call(
        _in_proj_kernel,
        grid=(n // tm,),
        in_specs=[row(D_MODEL), _const_spec((1, D_MODEL)), _const_spec(win.shape),
                  _const_spec((1, Q_LORA)), _const_spec((1, KV_LORA)), _const_spec(wuq.shape),
                  _const_spec(wukv.shape), _const_spec(ones.shape)] + [tab] * len(tables) + [
                  _const_spec((1, HG_W)), _const_spec((1, HG_W))],
        out_specs=[vt_spec if o is None else row(o[0]) for o in outs],
        out_shape=[vt_shape if o is None else jax.ShapeDtypeStruct((n, o[0]), o[1]) for o in outs],
        compiler_params=pltpu.CompilerParams(dimension_semantics=("parallel",), vmem_limit_bytes=VMEM_LIMIT),
        name="in_proj",
    )(x, gain, win, qn, kvn, wuq, wukv, ones, *tables, lbf, lbb)


def _attention_kernel(q_ref, k_ref, vt_ref, ot_out, p_scr, st_scr, acc_scr):
    nkb, _, kb = vt_ref.shape
    tq = p_scr.shape[2]
    steps = [(i, j) for i in range(q_ref.shape[0] // tq) for j in range(nkb)]

    def qk(n):
        i, j = steps[n]
        return _dot_nt(k_ref[j * kb:(j + 1) * kb, :], q_ref[i * tq:(i + 1) * tq, :])

    def finish_tile(i):
        acc = acc_scr[...]
        ot_out[:, i * tq:(i + 1) * tq] = (acc[:V_DIM] / acc[V_DIM:V_DIM + 1]).astype(BF16)
        return acc

    def stream(n, ref):
        st = qk(n)
        p_scr[n % 2] = jnp.exp2(st - ref).astype(BF16)
        return jnp.max(st, axis=0, keepdims=True)

    zero = jnp.zeros((1, tq), F32)
    ref = zero
    bmax = stream(0, ref)
    used = jnp.abs(bmax)
    for n, (i, j) in enumerate(steps):
        last = n + 1 == len(steps)
        if not last:
            first_of_tile = steps[n + 1][1] == 0
            next_ref = zero if first_of_tile else jnp.maximum(ref, bmax)
            next_bmax = stream(n + 1, next_ref)
            used = jnp.maximum(used, jnp.abs(next_bmax) if first_of_tile else next_bmax - next_ref)
        pv = _dot(vt_ref[j], p_scr[n % 2])
        acc_scr[...] = pv if j == 0 else acc_scr[...] + pv
        if j == nkb - 1:
            acc = finish_tile(i)
            not_finite = jnp.where(jnp.abs(acc) < jnp.inf, 0.0, jnp.inf)
            used = jnp.maximum(used, jnp.max(not_finite, axis=0, keepdims=True))
        elif not last:
            acc_scr[...] = acc_scr[...] * jnp.exp2(ref - next_ref)
        if not last:
            ref, bmax = next_ref, next_bmax

    @pl.when(jnp.logical_not(jnp.max(used) <= ATT_HEADROOM))
    def _():
        slots = st_scr.shape[0]
        for n in range(slots - 1):
            st_scr[n] = qk(n)
        m = None
        for n, (i, j) in enumerate(steps):
            if n + slots - 1 < len(steps):
                st_scr[(n + slots - 1) % slots] = qk(n + slots - 1)
            st = st_scr[n % slots]
            blk = jnp.max(st, axis=0, keepdims=True)
            m_new = blk if j == 0 else jnp.maximum(m, blk)
            pv = _dot(vt_ref[j], jnp.exp2(st - m_new).astype(BF16))
            acc_scr[...] = pv if j == 0 else jnp.exp2(m - m_new) * acc_scr[...] + pv
            m = m_new
            if j == nkb - 1:
                finish_tile(i)


def _attention(q, k, vt, *, tq):
    bsz, seq, _ = q.shape
    nkb = vt.shape[0] // bsz
    kb = vt.shape[2]
    head = pl.BlockSpec((None, seq, HEAD_SLAB), lambda b, h: (b, 0, h))
    return pl.pallas_call(
        _attention_kernel,
        grid=(bsz, ATT_HEADS),
        in_specs=[head, head, pl.BlockSpec((nkb, V_ROWS, kb), lambda b, h: (b, h, 0))],
        out_specs=pl.BlockSpec((None, V_DIM, seq), lambda b, h: (b, h, 0)),
        out_shape=jax.ShapeDtypeStruct((bsz, ATT_WIDTH, seq), BF16),
        scratch_shapes=[pltpu.VMEM((2, kb, tq), BF16), pltpu.VMEM((3, kb, tq), F32), pltpu.VMEM((V_ROWS, tq), F32)],
        compiler_params=pltpu.CompilerParams(dimension_semantics=("parallel", "parallel"),
                                             vmem_limit_bytes=VMEM_LIMIT),
        name="attention",
    )(q, k, vt)


_HG_LEVELS = int(np.log2(HG_CHUNK))
_HG_DIAG = _HG_LEVELS
_HG_SUBLANES = 8
_HG_UNROLL = 16


@functools.lru_cache(maxsize=None)
def _hgrn_consts():
    c = HG_CHUNK
    t = np.arange(c)[:, None]
    u = np.arange(c)[None, :]
    blocks = [u <= t]
    level = np.full((c, c), -1, np.int32)
    level[np.arange(c), np.arange(c)] = _HG_DIAG
    for li in range(_HG_LEVELS):
        h = (c // 2) >> li
        mid = (t // (2 * h)) * (2 * h) + h - 1
        upper = (t % (2 * h)) >= h
        if 1 < h < _HG_SUBLANES:
            blocks.append((upper & (u > mid) & (u <= t)) | (~upper & (u > t) & (u <= mid)))
        same = (t // (2 * h)) == (u // (2 * h))
        level[same & upper & ((u % (2 * h)) < h)] = li
    m_f = np.concatenate(blocks, 0).astype(np.float32)
    m_b = np.concatenate([blk[::-1, ::-1] for blk in blocks], 0).astype(np.float32)
    return m_f, m_b, level, np.ascontiguousarray(level.T)


def _hgrn_chunk(rows, backward, m_ref, lv_ref, q_ref, kk_ref, g_ref, v_ref, st):
    c = HG_CHUNK
    g = g_ref[rows, :]
    g_hi = g.astype(BF16)
    g_lo = (g - g_hi.astype(F32)).astype(BF16)
    r = _dot(m_ref[...], jnp.concatenate([g_hi, g_lo], axis=1))
    x = r[:, :HG_KEY] + r[:, HG_KEY:]
    b = x[0:c]
    q = q_ref[rows, :].astype(F32)
    kk = kk_ref[rows, :].astype(F32)
    v = v_ref[rows, :]
    lv = lv_ref[...]
    tbit = lax.broadcasted_iota(jnp.int32, (c, HG_KEY), 0)

    tile = _HG_SUBLANES
    lvr = [lv[r * tile:(r + 1) * tile] for r in range(c // tile)]
    diag = jnp.sum(q * kk, axis=-1, keepdims=True)
    srows = [jnp.where(lvr[r] == _HG_DIAG, diag[r * tile:(r + 1) * tile], 0.0) for r in range(c // tile)]
    kk_prev = pltpu.roll(kk, (c - 1) if backward else 1, axis=0)
    near = jnp.sum(q * kk_prev * jnp.exp2(g), axis=-1, keepdims=True)
    small = 1
    for li in range(_HG_LEVELS):
        h = (c // 2) >> li
        if h == 1:
            for r in range(c // tile):
                srows[r] = jnp.where(lvr[r] == li, near[r * tile:(r + 1) * tile], srows[r])
            continue
        if h >= tile:
            parts, qparts, qrows = [], [], []
            for s0 in range(0, c, 2 * h):
                lo, hi = slice(s0, s0 + h), slice(s0 + h, s0 + 2 * h)
                if backward:
                    bm = b[s0 + h:s0 + h + 1]
                    pq, pk = q[lo] * jnp.exp2(b[lo] - bm), kk[hi] * jnp.exp2(bm - b[hi])
                    parts += [pq, pk]
                    qrows += range(s0 // tile, (s0 + h) // tile)
                else:
                    bm = b[s0 + h - 1:s0 + h]
                    pk, pq = kk[lo] * jnp.exp2(bm - b[lo]), q[hi] * jnp.exp2(b[hi] - bm)
                    parts += [pk, pq]
                    qrows += range((s0 + h) // tile, (s0 + 2 * h) // tile)
                qparts.append(pq)
            gram = _dot_nt(jnp.concatenate(qparts, axis=0).astype(BF16),
                           jnp.concatenate(parts, axis=0).astype(BF16))
        else:
            is_query = ((tbit & h) == 0) if backward else ((tbit & h) != 0)
            p = (jnp.where(is_query, q, kk) * jnp.exp2(x[small * c:(small + 1) * c])).astype(BF16)
            small += 1
            gram = _dot_nt(p, p)
            qrows = range(c // tile)
        for i, r in enumerate(qrows):
            srows[r] = jnp.where(lvr[r] == li, gram[i * tile:(i + 1) * tile], srows[r])
    scores = jnp.concatenate(srows, axis=0)

    b_end = b[0:1] if backward else b[c - 1:c]
    o = _dot(scores.astype(BF16), v)
    o = o + _dot_nt((q * jnp.exp2(b)).astype(BF16), st.astype(BF16))
    kd = (kk * jnp.exp2(b_end - b)).astype(BF16)
    st_new = st * jnp.exp2(b_end) + _dot_tn(v, kd)
    return o, st_new


def _hgrn_kernel(mf_ref, mb_ref, lvf_ref, lvb_ref, q_ref, kkf_ref, kkb_ref, gf_ref, gb_ref, v_ref,
                 gate_ref, norm_ref, r_out, acc):
    c = HG_CHUNK
    nc = q_ref.shape[0] // c
    half = nc // 2

    def both(i, st_f, st_b):
        rows_f = pl.ds(pl.multiple_of(i * c, c), c)
        rows_b = pl.ds(pl.multiple_of((nc - 1 - i) * c, c), c)
        o_f, st_f = _hgrn_chunk(rows_f, False, mf_ref, lvf_ref, q_ref, kkf_ref, gf_ref, v_ref, st_f)
        o_b, st_b = _hgrn_chunk(rows_b, True, mb_ref, lvb_ref, q_ref, kkb_ref, gb_ref, v_ref, st_b)
        return rows_f, rows_b, o_f, o_b, st_f, st_b

    def first_touch(i, carry):
        rows_f, rows_b, o_f, o_b, st_f, st_b = both(i, *carry)
        acc[rows_f, :] = o_f
        acc[rows_b, :] = o_b
        return st_f, st_b

    def finish(rows, o):
        tot = acc[rows, :] + o
        r_out[rows, :] = (_rms(tot, norm_ref[...]) * gate_ref[rows, :].astype(F32)).astype(BF16)

    def second_touch(i, carry):
        rows_f, rows_b, o_f, o_b, st_f, st_b = both(i, *carry)
        finish(rows_f, o_f)
        finish(rows_b, o_b)
        return st_f, st_b

    zero = jnp.zeros((HG_VAL, HG_KEY), F32)
    carry = lax.fori_loop(0, half, first_touch, (zero, zero), unroll=_HG_UNROLL)
    lax.fori_loop(half, nc, second_touch, carry, unroll=_HG_UNROLL)


def _hgrn(qs, kkf, kkb, gf, gb, hv, gate, norm):
    bsz, seq, _ = qs.shape
    assert seq % (2 * _HG_UNROLL * HG_CHUNK) == 0
    m_f, m_b, lv_f, lv_b = _hgrn_consts()
    head = pl.BlockSpec((None, seq, HG_KEY), lambda b, h: (b, 0, h))
    return pl.pallas_call(
        _hgrn_kernel,
        grid=(bsz, HG_HEADS),
        in_specs=[_const_spec(m_f.shape), _const_spec(m_b.shape), _const_spec(lv_f.shape), _const_spec(lv_b.shape),
                  head, head, head, head, head, head, head,
                  pl.BlockSpec((1, HG_VAL), lambda b, h: (0, h))],
        out_specs=head,
        out_shape=jax.ShapeDtypeStruct((bsz, seq, HG_W), BF16),
        scratch_shapes=[pltpu.VMEM((seq, HG_VAL), F32)],
        compiler_params=pltpu.CompilerParams(dimension_semantics=("parallel", "parallel"),
                                             vmem_limit_bytes=VMEM_LIMIT),
        name="hgrn",
    )(jnp.asarray(m_f, BF16), jnp.asarray(m_b, BF16), jnp.asarray(lv_f), jnp.asarray(lv_b),
      qs, kkf, kkb, gf, gb, hv, gate, norm)


_FF_CHUNK = 1024


def _post_kernel(x_ref, at_ref, r_ref, p_ref, na_ref, wo_ref, n1_ref, n2_ref, wup_ref, wdn_ref, n3_ref,
                 wple_ref, wpg_ref, n4_ref, x_out):
    a = _rms(at_ref[...].astype(F32).T, na_ref[...]).astype(BF16)
    mix = _dot(a, wo_ref[:ATT_WIDTH, :]) + _dot(r_ref[...], wo_ref[ATT_WIDTH:, :])
    x = x_ref[...] + _rms(mix, n1_ref[...])
    h = _rms(x, n2_ref[...]).astype(BF16)
    m = jnp.zeros_like(x)
    for c0 in range(0, D_FF, _FF_CHUNK):
        act = jnp.square(jnp.maximum(_dot(h, wup_ref[:, c0:c0 + _FF_CHUNK]), 0.0)).astype(BF16)
        m = m + _dot(act, wdn_ref[c0:c0 + _FF_CHUNK, :])
    x = x + _rms(m, n3_ref[...])
    e = _dot(p_ref[...].astype(BF16), wple_ref[...])
    gate = jax.nn.sigmoid(_dot(x.astype(BF16), wpg_ref[...]))
    x_out[...] = x + _rms(e * gate, n4_ref[...])


def _post(x, at, r, p, layer, na, wo, n1, n2, wup, wdn, n3, wple, wpg, n4, *, tm):
    n = x.shape[0]
    nt = at.shape[2] // tm
    row = lambda w: pl.BlockSpec((tm, w), lambda i: (i, 0))
    at_spec = pl.BlockSpec((None, ATT_WIDTH, tm), lambda i: (i // nt, 0, i % nt))
    vec = _const_spec((1, D_MODEL))
    single = lambda w: pl.BlockSpec(w.shape, lambda i: (0, 0), pipeline_mode=pl.Buffered(1))
    return pl.pallas_call(
        _post_kernel,
        grid=(n // tm,),
        in_specs=[row(D_MODEL), at_spec, row(HG_W), pl.BlockSpec((None, tm, PLE_DIM), lambda i: (layer, i, 0)),
                  _const_spec((1, ATT_WIDTH)), single(wo), vec, vec,
                  single(wup), single(wdn), vec, single(wple), single(wpg), vec],
        out_specs=row(D_MODEL),
        out_shape=jax.ShapeDtypeStruct((n, D_MODEL), F32),
        compiler_params=pltpu.CompilerParams(dimension_semantics=("parallel",), vmem_limit_bytes=VMEM_LIMIT),
        name="post",
    )(x, at, r, p, na, wo, n1, n2, wup, wdn, n3, wple, wpg, n4)


def _layout_w_in(w):
    head = w[:, :Q_LORA + KV_LORA + ROPE_DIM]
    rest = w[:, Q_LORA + KV_LORA + ROPE_DIM:]
    pad = jnp.zeros((w.shape[0], LANES - ROPE_DIM), w.dtype)
    return jnp.concatenate([head, pad, rest], axis=1)


def _layout_w_uq(w):
    w = w.reshape(Q_LORA, ATT_HEADS, QK_DIM)
    pad = jnp.zeros((Q_LORA, ATT_HEADS, HEAD_SLAB - QK_DIM), w.dtype)
    return jnp.concatenate([w, pad], axis=-1).reshape(Q_LORA, QK_SLAB)


def _layout_w_ukv(w):
    w = w.reshape(KV_LORA, ATT_HEADS, NOPE_DIM + V_DIM)
    pad = jnp.zeros((KV_LORA, ATT_HEADS, HEAD_SLAB - NOPE_DIM), w.dtype)
    k = jnp.concatenate([w[..., :NOPE_DIM], pad], axis=-1).reshape(KV_LORA, QK_SLAB)
    vpad = jnp.zeros((KV_LORA, ATT_HEADS, V_ROWS - V_DIM), w.dtype)
    v = jnp.concatenate([w[..., NOPE_DIM:], vpad], axis=-1).reshape(KV_LORA, VT_ROWS)
    return jnp.concatenate([k, v], axis=1)


def _value_ones():
    o = np.zeros((ATT_HEADS, V_ROWS), np.float32)
    o[:, V_DIM:] = 1.0
    return o.reshape(1, VT_ROWS)


def _rope_tables(seq):
    inv = 1.0 / (ROPE_THETA ** (jnp.arange(0, ROPE_DIM, 2, dtype=F32) / ROPE_DIM))
    ang = jnp.arange(seq, dtype=F32)[:, None] * inv[None, :]
    cos, sin = jnp.cos(ang), jnp.sin(ang)
    scale = jnp.asarray(QK_DIM ** -0.5 * LOG2E, F32)
    z = lambda w: jnp.zeros((seq, w), F32)
    zs = jnp.zeros_like(sin)
    cq = jnp.concatenate([jnp.ones((seq, NOPE_DIM), F32), cos, cos, z(HEAD_SLAB - QK_DIM)], axis=1) * scale
    sqa = jnp.concatenate([z(NOPE_DIM), -sin, zs, z(HEAD_SLAB - QK_DIM)], axis=1) * scale
    sqb = jnp.concatenate([z(NOPE_DIM), zs, sin, z(HEAD_SLAB - QK_DIM)], axis=1) * scale
    ck = jnp.concatenate([cos, cos, z(LANES - ROPE_DIM)], axis=1)
    ska = jnp.concatenate([-sin, zs, z(LANES - ROPE_DIM)], axis=1)
    skb = jnp.concatenate([zs, sin, z(LANES - ROPE_DIM)], axis=1)
    return cq, sqa, sqb, ck, ska, skb


def _trunk(x, p, lbs, norm_mix_pre, w_in, q_norm, w_uq, kv_norm, w_ukv, att_out_norm, hg_out_norm, w_o,
           norm_mix_post, norm_mlp_pre, w_up, w_down, norm_mlp_post, w_ple, w_ple_gate, norm_ple):
    bsz, seq, _ = x.shape
    n = bsz * seq
    tm = min(512, seq)
    tq = min(512, seq)
    tables = _rope_tables(seq)
    ones = jnp.asarray(_value_ones())
    row = lambda v: v.reshape(1, -1).astype(F32)
    xf = x.reshape(n, D_MODEL)
    for i in range(DEPTH):
        q, k, vt, hq, kkf, kkb, gf, gb, hv, gate = _in_proj(
            xf, row(norm_mix_pre[i]), _layout_w_in(w_in[i]).astype(BF16), row(q_norm[i]), row(kv_norm[i]),
            _layout_w_uq(w_uq[i]).astype(BF16), _layout_w_ukv(w_ukv[i]).astype(BF16), ones,
            tables, row(lbs[0, i]), row(lbs[1, i]), seq=seq, tm=tm)
        b3 = lambda t: t.reshape(bsz, seq, t.shape[-1])
        at = _attention(b3(q), b3(k), vt, tq=tq)
        r = _hgrn(b3(hq), b3(kkf), b3(kkb), b3(gf), b3(gb), b3(hv), b3(gate), row(hg_out_norm[i]))
        xf = _post(xf, at, r.reshape(n, HG_W), p.reshape(DEPTH, n, PLE_DIM), i, row(att_out_norm[i]),
                   w_o[i].astype(BF16), row(norm_mix_post[i]), row(norm_mlp_pre[i]), w_up[i].astype(BF16),
                   w_down[i].astype(BF16), row(norm_mlp_post[i]), w_ple[i].astype(BF16),
                   w_ple_gate[i].astype(BF16), row(norm_ple[i]), tm=min(2 * tm, seq))
    return xf.reshape(bsz, seq, D_MODEL)


def kernel(x_prompt, x_sample, p_prompt, p_sample, norm_mix_pre, w_in, q_norm, w_uq, kv_norm, w_ukv, att_out_norm, hg_lb, hg_out_norm, w_o, norm_mix_post, norm_mlp_pre, w_up, w_down, norm_mlp_post, w_ple, w_ple_gate, norm_ple):
    lbs = jnp.cumsum(jax.nn.softmax(hg_lb.astype(F32), axis=1), axis=1)
    lbs = lbs - lbs[:, :1]
    ws = (norm_mix_pre, w_in, q_norm, w_uq, kv_norm, w_ukv, att_out_norm, hg_out_norm, w_o, norm_mix_post,
          norm_mlp_pre, w_up, w_down, norm_mlp_post, w_ple, w_ple_gate, norm_ple)
    return (_trunk(x_prompt, p_prompt, lbs, *ws), _trunk(x_sample, p_sample, lbs, *ws))
```
